```python
import jax
import jax.numpy as jnp
from jax import lax
import numpy as np

D_MODEL = 1024
BATCH = 4
SEQ = 4096
DEPTH = 4
DEC_BATCH = 128
DEC_SEQ = 4
PAST_LEN = 2048
PAGE_SIZE = 128

N_MIXERS = 2
CONV_EXPAND = 2
D_CONV = CONV_EXPAND * D_MODEL
CONV_WIDTH = 31
CONV_STATE = CONV_WIDTH - 1
N_HEADS = 16
HEAD_DIM = D_MODEL // N_HEADS
N_KV_GROUPS = 4
HEADS_PER_GROUP = N_HEADS // N_KV_GROUPS
D_ATTN = N_HEADS * HEAD_DIM
D_KV = N_KV_GROUPS * HEAD_DIM
CMP_BLOCK = 32
CMP_STRIDE = 16
CMP_HIDDEN = 2 * HEAD_DIM
SEL_BLOCK = 64
N_SELECT = 16
WINDOW = 512
N_BRANCH = 3
N_KV_SLOTS = 6
N_PAGED_SLOTS = 4
Q_BLOCK = 64
NSA_IN = 2 * D_ATTN + N_KV_SLOTS * D_KV + N_BRANCH * N_HEADS
N_CONV_LAYERS = (DEPTH + 1) // 2
N_NSA_LAYERS = DEPTH // 2
NORM_EPS = 1e-6
FORCE_SCORE = 1e4

kernel_name = "conformer_conv_nsa_hybrid_step"


def rms_norm(x, g):
    xf = x.astype(jnp.float32)
    y = xf * lax.rsqrt(jnp.mean(xf * xf, axis=-1, keepdims=True) + NORM_EPS)
    return (y * g.astype(jnp.float32)).astype(x.dtype)


def layer_norm(x, g, b):
    xf = x.astype(jnp.float32)
    xc = xf - jnp.mean(xf, axis=-1, keepdims=True)
    var = jnp.mean(xc * xc, axis=-1, keepdims=True)
    y = xc * lax.rsqrt(var + NORM_EPS) * g.astype(jnp.float32) + b.astype(jnp.float32)
    return y.astype(x.dtype)


def alibi_slopes():
    h = jnp.arange(1, N_HEADS + 1, dtype=jnp.float32)
    return jnp.exp2(-8.0 * h / N_HEADS).reshape(1, 1, N_KV_GROUPS, HEADS_PER_GROUP, 1)


def masked_softmax(s, mask):
    s = jnp.where(mask, s, -jnp.inf)
    m = jnp.max(s, axis=-1, keepdims=True)
    m = jnp.where(jnp.isfinite(m), m, 0.0)
    e = jnp.where(mask, jnp.exp(s - m), 0.0)
    return e / jnp.maximum(jnp.sum(e, axis=-1, keepdims=True), 1e-30)


def conv_mixer(h, buf, w_in, dw_w, dw_b, ln_g, ln_b, w_out):
    a, gl, z = jnp.split(h @ w_in, 3, axis=-1)
    v = a * jax.nn.sigmoid(gl)
    vp = jnp.concatenate([buf.astype(v.dtype), v], axis=1)
    c = lax.conv_general_dilated(vp, dw_w[:, None, :].astype(vp.dtype), window_strides=(1,), padding="VALID",
                                 dimension_numbers=("NWC", "WIO", "NWC"), feature_group_count=D_CONV) + dw_b
    c = jax.nn.silu(layer_norm(c, ln_g, ln_b))
    y = (c * jax.nn.silu(z)) @ w_out
    return y, vp[:, -CONV_STATE:]


def nsa_project(h, w_in):
    b, t = h.shape[:2]
    u = h @ w_in
    q, z, kv, gl = jnp.split(u, [D_ATTN, 2 * D_ATTN, 2 * D_ATTN + N_KV_SLOTS * D_KV], axis=-1)
    return (q.reshape(b, t, N_HEADS, HEAD_DIM), z,
            kv.reshape(b, t, N_KV_SLOTS, N_KV_GROUPS, HEAD_DIM),
            gl.reshape(b, t, N_HEADS, N_BRANCH))


def compress(k, pe, w1, w2):
    n_cmp = (k.shape[1] - CMP_BLOCK) // CMP_STRIDE + 1
    idx = jnp.arange(n_cmp)[:, None] * CMP_STRIDE + jnp.arange(CMP_BLOCK)[None, :]
    kb = k[:, idx] + pe[:, None, :]
    hid = jax.nn.silu(jnp.einsum("bclgd,ldh->bcgh", kb, w1))
    c_end = jnp.arange(n_cmp) * CMP_STRIDE + (CMP_BLOCK - 1)
    return jnp.einsum("bcgh,hd->bcgd", hid, w2), c_end


def to_blocks(k):
    b, t = k.shape[:2]
    return k.reshape(b, t // SEL_BLOCK, SEL_BLOCK, N_KV_GROUPS, HEAD_DIM).transpose(0, 3, 1, 2, 4)


def nsa_core(q, gate_logit, q_pos, kc, vc, c_end, ks_blk, vs_blk, kw, vw, w_pos):
    b, nq = q.shape[:2]
    n_sb = ks_blk.shape[2]
    scale = HEAD_DIM ** -0.5
    slopes = alibi_slopes()
    qg = q.reshape(b, nq, N_KV_GROUPS, HEADS_PER_GROUP, HEAD_DIM)

    dist_c = q_pos[:, None] - c_end[None, :]
    s_c = (jnp.einsum("bqghd,bcgd->bqghc", qg, kc).astype(jnp.float32) * scale
           - slopes * dist_c[None, :, None, None, :].astype(jnp.float32))
    p_c = masked_softmax(s_c, (dist_c >= 0)[None, :, None, None, :])
    o_c = jnp.einsum("bqghc,bcgd->bqghd", p_c.astype(vc.dtype), vc)

    blk = jnp.arange(n_sb)
    c_start = c_end - (CMP_BLOCK - 1)
    overlap = ((c_start[:, None] < (blk[None, :] + 1) * SEL_BLOCK)
               & (c_end[:, None] >= blk[None, :] * SEL_BLOCK)).astype(jnp.float32)
    imp = jnp.einsum("bqghc,cj->bqgj", p_c, overlap)
    cur = (q_pos // SEL_BLOCK)[:, None]
    forced = (blk[None, :] == 0) | (blk[None, :] == cur) | (blk[None, :] == cur - 1)
    eligible = blk[None, :] <= cur
    score = jnp.where(forced[None, :, None, :], FORCE_SCORE, imp)
    score = jnp.where(eligible[None, :, None, :], score, -jnp.inf)
    top_s, top_i = lax.top_k(score, min(N_SELECT, n_sb))
    bi = jnp.arange(b)[:, None, None, None]
    gi = jnp.arange(N_KV_GROUPS)[None, None, :, None]
    k_sel = ks_blk[bi, gi, top_i]
    v_sel = vs_blk[bi, gi, top_i]
    n_keys = top_i.shape[-1] * SEL_BLOCK
    key_pos = top_i[..., None] * SEL_BLOCK + jnp.arange(SEL_BLOCK)
    dist_s = q_pos[None, :, None, None, None] - key_pos
    s_s = (jnp.einsum("bqghd,bqgnld->bqghnl", qg, k_sel).astype(jnp.float32) * scale
           - slopes[..., None] * dist_s[:, :, :, None].astype(jnp.float32))
    mask_s = (jnp.isfinite(top_s)[..., None] & (dist_s >= 0)).reshape(b, nq, N_KV_GROUPS, 1, n_keys)
    p_s = masked_softmax(s_s.reshape(b, nq, N_KV_GROUPS, HEADS_PER_GROUP, n_keys), mask_s)
    o_s = jnp.einsum("bqghm,bqgmd->bqghd", p_s.astype(v_sel.dtype),
                     v_sel.reshape(b, nq, N_KV_GROUPS, n_keys, HEAD_DIM))

    dist_w = q_pos[:, None] - w_pos[None, :]
    s_w = (jnp.einsum("bqghd,bwgd->bqghw", qg, kw).astype(jnp.float32) * scale
           - slopes * dist_w[None, :, None, None, :].astype(jnp.float32))
    mask_w = ((dist_w >= 0) & (dist_w <= WINDOW) & (w_pos[None, :] >= 0))[None, :, None, None, :]
    p_w = masked_softmax(s_w, mask_w)
    o_w = jnp.einsum("bqghw,bwgd->bqghd", p_w.astype(vw.dtype), vw)

    g = jax.nn.sigmoid(gate_logit.astype(jnp.float32)).reshape(b, nq, N_KV_GROUPS, HEADS_PER_GROUP, N_BRANCH, 1)
    o = g[..., 0, :] * o_c + g[..., 1, :] * o_s + g[..., 2, :] * o_w
    return o.reshape(b, nq, D_ATTN).astype(q.dtype)


def nsa_prompt(h, w_in, w_out, pe, w1, w2):
    bsz, t = h.shape[:2]
    q, z, kv, gl = nsa_project(h, w_in)
    kc, c_end = compress(kv[:, :, 0], pe[0], w1[0], w2[0])
    vc, _ = compress(kv[:, :, 1], pe[1], w1[1], w2[1])
    ks_blk = to_blocks(kv[:, :, 2])
    vs_blk = to_blocks(kv[:, :, 3])
    kvw = jnp.concatenate([jnp.zeros((bsz, WINDOW, 2, N_KV_GROUPS, HEAD_DIM), kv.dtype), kv[:, :, 4:6]], axis=1)
    n_qb = t // Q_BLOCK
    qb = q.reshape(bsz, n_qb, Q_BLOCK, N_HEADS, HEAD_DIM).swapaxes(0, 1)
    glb = gl.reshape(bsz, n_qb, Q_BLOCK, N_HEADS, N_BRANCH).swapaxes(0, 1)

    def block(args):
        i, q_i, g_i = args
        qs = i * Q_BLOCK
        q_pos = qs + jnp.arange(Q_BLOCK)
        w = lax.dynamic_slice_in_dim(kvw, qs, WINDOW + Q_BLOCK, axis=1)
        w_pos = qs - WINDOW + jnp.arange(WINDOW + Q_BLOCK)
        return nsa_core(q_i, g_i, q_pos, kc, vc, c_end, ks_blk, vs_blk, w[:, :, 0], w[:, :, 1], w_pos)

    o = lax.map(block, (jnp.arange(n_qb), qb, glb)).swapaxes(0, 1).reshape(bsz, t, D_ATTN)
    y = (o * jax.nn.silu(z)) @ w_out
    return y, kv[:, :, :N_PAGED_SLOTS], kv[:, t - min(WINDOW, t):, 4:6]


def nsa_sample(h, cache_kv, li, page_table, win_buf, w_in, w_out, pe, w1, w2):
    s = h.shape[1]
    past = page_table.shape[1] * PAGE_SIZE
    q, z, kv, gl = nsa_project(h, w_in)
    q_pos = past + jnp.arange(s)
    total = past + s
    n_sb = -(-total // SEL_BLOCK)
    sel_pad = n_sb * SEL_BLOCK - total
    win_keep = win_buf.shape[1]
    w_pos = past - win_keep + jnp.arange(win_keep + s)

    def one_seq(args):
        pages, kv_i, q_i, g_i, wb_i = args
        past_kv = cache_kv[li, pages].reshape(past, N_PAGED_SLOTS, N_KV_GROUPS, HEAD_DIM)
        full = jnp.concatenate([past_kv.astype(kv_i.dtype), kv_i[:, :N_PAGED_SLOTS]], axis=0)[None]
        kc, c_end = compress(full[:, :, 0], pe[0], w1[0], w2[0])
        vc, _ = compress(full[:, :, 1], pe[1], w1[1], w2[1])
        sel = jnp.pad(full[:, :, 2:4], ((0, 0), (0, sel_pad), (0, 0), (0, 0), (0, 0)))
        w = jnp.concatenate([wb_i.astype(kv_i.dtype), kv_i[:, 4:6]], axis=0)[None]
        o = nsa_core(q_i[None], g_i[None], q_pos, kc, vc, c_end, to_blocks(sel[:, :, 0]), to_blocks(sel[:, :, 1]),
                     w[:, :, 0], w[:, :, 1], w_pos)
        return o[0]

    o = lax.map(one_seq, (page_table, kv, q, gl, win_buf))
    y = (o * jax.nn.silu(z)) @ w_out
    new_win = jnp.concatenate([win_buf.astype(kv.dtype), kv[:, :, 4:6]], axis=1)[:, -win_keep:]
    return y, kv[:, :, :N_PAGED_SLOTS], new_win


def setup_inputs(seed: int = 0) -> dict:
    key = jax.random.key(seed)
    ks = jax.random.split(key, 24)
    nrm = lambda k, shape, sc: jax.random.normal(k, shape, jnp.float32) * sc
    n_pages = PAST_LEN // PAGE_SIZE
    n_used = DEC_BATCH * n_pages
    n_pool = (5 * n_used + 3) // 4
    perm = jax.random.permutation(ks[0], n_pool)
    page_table = perm[:n_used].reshape(DEC_BATCH, n_pages).astype(jnp.int32)
    win_keep = min(WINDOW, PAST_LEN)
    return {
        "x_prompt": nrm(ks[1], (BATCH, SEQ, D_MODEL), 1.0),
        "x_sample": nrm(ks[2], (DEC_BATCH, DEC_SEQ, D_MODEL), 1.0),
        "cache_kv": nrm(ks[3], (N_NSA_LAYERS, n_pool, PAGE_SIZE, N_PAGED_SLOTS, N_KV_GROUPS, HEAD_DIM), 1.0),
        "state_kv_win": nrm(ks[4], (N_NSA_LAYERS, DEC_BATCH, win_keep, 2, N_KV_GROUPS, HEAD_DIM), 1.0),
        "state_conv": nrm(ks[5], (N_CONV_LAYERS, DEC_BATCH, CONV_STATE, D_CONV), 0.5),
        "page_table": page_table,
        "norm_g": 1.0 + nrm(ks[6], (DEPTH, D_MODEL), 0.02),
        "final_norm_g": 1.0 + nrm(ks[7], (D_MODEL,), 0.02),
        "conv_w_in": nrm(ks[8], (N_CONV_LAYERS, D_MODEL, 3 * D_CONV), D_MODEL ** -0.5),
        "conv_dw_w": nrm(ks[9], (N_CONV_LAYERS, CONV_WIDTH, D_CONV), CONV_WIDTH ** -0.5),
        "conv_dw_b": nrm(ks[10], (N_CONV_LAYERS, D_CONV), 0.02),
        "conv_ln_g": 1.0 + nrm(ks[11], (N_CONV_LAYERS, D_CONV), 0.02),
        "conv_ln_b": nrm(ks[12], (N_CONV_LAYERS, D_CONV), 0.02),
        "conv_w_out": nrm(ks[13], (N_CONV_LAYERS, D_CONV, D_MODEL), D_CONV ** -0.5),
        "nsa_w_in": nrm(ks[14], (N_NSA_LAYERS, D_MODEL, NSA_IN), D_MODEL ** -0.5),
        "nsa_w_out": nrm(ks[15], (N_NSA_LAYERS, D_ATTN, D_MODEL), D_ATTN ** -0.5),
        "nsa_cmp_pe": nrm(ks[16], (N_NSA_LAYERS, 2, CMP_BLOCK, HEAD_DIM), 0.1),
        "nsa_cmp_w1": nrm(ks[17], (N_NSA_LAYERS, 2, CMP_BLOCK, HEAD_DIM, CMP_HIDDEN), (CMP_BLOCK * HEAD_DIM) ** -0.5),
        "nsa_cmp_w2": nrm(ks[18], (N_NSA_LAYERS, 2, CMP_HIDDEN, HEAD_DIM), CMP_HIDDEN ** -0.5),
    }


def reference(x_prompt, x_sample, cache_kv, state_kv_win, state_conv, page_table, norm_g, final_norm_g,
              conv_w_in, conv_dw_w, conv_dw_b, conv_ln_g, conv_ln_b, conv_w_out,
              nsa_w_in, nsa_w_out, nsa_cmp_pe, nsa_cmp_w1, nsa_cmp_w2):
    xp, xs = x_prompt, x_sample
    kv_p, kv_s, win_p, win_s, conv_p, conv_s = [], [], [], [], [], []
    for layer in range(DEPTH):
        j = layer // N_MIXERS
        hp = rms_norm(xp, norm_g[layer])
        hs = rms_norm(xs, norm_g[layer])
        if layer % N_MIXERS == 0:
            conv_args = (conv_w_in[j], conv_dw_w[j], conv_dw_b[j], conv_ln_g[j], conv_ln_b[j], conv_w_out[j])
            zero_buf = jnp.zeros((xp.shape[0], CONV_STATE, D_CONV), xp.dtype)
            yp, cp = conv_mixer(hp, zero_buf, *conv_args)
            ys, cs = conv_mixer(hs, state_conv[j], *conv_args)
            conv_p.append(cp)
            conv_s.append(cs)
        else:
            nsa_args = (nsa_w_in[j], nsa_w_out[j], nsa_cmp_pe[j], nsa_cmp_w1[j], nsa_cmp_w2[j])
            yp, kp, wp = nsa_prompt(hp, *nsa_args)
            ys, kq, wq = nsa_sample(hs, cache_kv, j, page_table, state_kv_win[j], *nsa_args)
            kv_p.append(kp)
            kv_s.append(kq)
            win_p.append(wp)
            win_s.append(wq)
        xp = xp + yp
        xs = xs + ys
    y_prompt = rms_norm(xp, final_norm_g)
    y_sample = rms_norm(xs, final_norm_g)
    kv_prompt = jnp.stack(kv_p)
    kv_sample = jnp.stack(kv_s)
    win_prompt = jnp.stack(win_p)
    win_sample = jnp.stack(win_s)
    conv_prompt = jnp.stack(conv_p)
    conv_sample = jnp.stack(conv_s)
    return (y_prompt, y_sample, kv_prompt, kv_sample, win_prompt, win_sample, conv_prompt, conv_sample)
```

```python
import functools

import jax
import jax.numpy as jnp
from jax import lax
from jax.experimental import pallas as pl
from jax.experimental.pallas import tpu as pltpu

F32 = jnp.float32
BF16 = jnp.bfloat16
I32 = jnp.int32

D_MODEL = 1024
DEPTH = 4
DEC_SEQ = 4
PAGE_SIZE = 128
D_CONV = 2048
CONV_WIDTH = 31
CONV_STATE = CONV_WIDTH - 1
N_HEADS = 16
HEAD_DIM = 64
N_KV_GROUPS = 4
HEADS_PER_GROUP = 4
D_KV = N_KV_GROUPS * HEAD_DIM
CMP_BLOCK = 32
CMP_STRIDE = 16
CMP_HIDDEN = 128
SEL_BLOCK = 64
N_SELECT = 16
WINDOW = 512
N_BRANCH = 3
NSA_IN = 2 * D_MODEL + 6 * D_KV + N_BRANCH * N_HEADS
NSA_IN_PAD = 3712
NORM_EPS = 1e-6
FORCE_SCORE = 1e4
NEG = -1e30
SPAD = 8
LANES = 128
VMEM_LIMIT = 56 * 1024 * 1024


def _params(*sem):
    return pltpu.CompilerParams(dimension_semantics=sem, vmem_limit_bytes=VMEM_LIMIT)


def _rms(x, g):
    return x * lax.rsqrt(jnp.mean(x * x, axis=-1, keepdims=True) + NORM_EPS) * g


def _silu(x):
    return x * jax.nn.sigmoid(x)


def _dot(a, b):
    return jnp.dot(a, b, preferred_element_type=F32)


def _dot_nt(a, b):
    return lax.dot_general(a, b, (((1,), (1,)), ((), ())), preferred_element_type=F32)


def _rep(x, n):
    return jnp.concatenate([x] * n, axis=1) if n > 1 else x


def _conv_inproj_body(x_ref, g_ref, w_ref, v_ref, zg_ref):
    h = _rms(x_ref[...], g_ref[...]).astype(BF16)
    ch = 512
    for c in range(D_CONV // ch):
        lo = c * ch
        a = _dot(h, w_ref[:, lo:lo + ch])
        gl = _dot(h, w_ref[:, D_CONV + lo:D_CONV + lo + ch])
        z = _dot(h, w_ref[:, 2 * D_CONV + lo:2 * D_CONV + lo + ch])
        v_ref[:, lo:lo + ch] = a * jax.nn.sigmoid(gl)
        zg_ref[:, lo:lo + ch] = _silu(z)


def conv_inproj(x, g, w_bf, tm=256):
    n = x.shape[0]
    return pl.pallas_call(
        _conv_inproj_body,
        grid=(n // tm,),
        in_specs=[pl.BlockSpec((tm, D_MODEL), lambda i: (i, 0)),
                  pl.BlockSpec((1, D_MODEL), lambda i: (0, 0)),
                  pl.BlockSpec((D_MODEL, 3 * D_CONV), lambda i: (0, 0))],
        out_specs=[pl.BlockSpec((tm, D_CONV), lambda i: (i, 0)),
                   pl.BlockSpec((tm, D_CONV), lambda i: (i, 0))],
        out_shape=[jax.ShapeDtypeStruct((n, D_CONV), F32),
                   jax.ShapeDtypeStruct((n, D_CONV), F32)],
        compiler_params=_params("arbitrary"),
        name="conv_inproj",
    )(x, g, w_bf)


_HDR = 32
_RW = 32
_CW = 128


def _ln_gate(c, lg, lb, zg):
    mu = jnp.mean(c, axis=-1, keepdims=True)
    xc = c - mu
    var = jnp.mean(xc * xc, axis=-1, keepdims=True)
    y = xc * lax.rsqrt(var + NORM_EPS) * lg + lb
    return _silu(y) * zg


def _conv_prompt_body(v_ref, zg_ref, x_ref, dw_ref, db_ref, lg_ref, lb_ref, wout_ref, o_ref,
                      vbuf, cbuf, ybuf, *, tt):
    i = pl.program_id(1)

    @pl.when(i == 0)
    def _():
        vbuf[0:_HDR, :] = jnp.zeros((_HDR, D_CONV), F32)

    @pl.when(i > 0)
    def _():
        vbuf[0:_HDR, :] = vbuf[tt:tt + _HDR, :]

    vbuf[_HDR:_HDR + tt, :] = v_ref[...]
    off = _HDR - CONV_STATE

    def row_body(r, carry):
        r0 = pl.multiple_of(r * _RW, _RW)

        def col_body(c, carry2):
            c0 = pl.multiple_of(c * _CW, _CW)
            big = vbuf[pl.ds(r0, _RW + _HDR), pl.ds(c0, _CW)]
            shifted = [big] + [big[b:b + _RW + 24] for b in range(1, 8)]
            acc = jnp.zeros((_RW, _CW), F32)
            for j in range(CONV_WIDTH):
                a, b = divmod(j + off, 8)
                tap = shifted[b][8 * a:8 * a + _RW]
                acc = acc + tap * dw_ref[pl.ds(j, 1), pl.ds(c0, _CW)]
            cbuf[pl.ds(r0, _RW), pl.ds(c0, _CW)] = acc + db_ref[:, pl.ds(c0, _CW)]
            return carry2

        lax.fori_loop(0, D_CONV // _CW, col_body, 0)
        y = _ln_gate(cbuf[pl.ds(r0, _RW), :], lg_ref[...], lb_ref[...], zg_ref[pl.ds(r0, _RW), :])
        ybuf[pl.ds(r0, _RW), :] = y.astype(BF16)
        return carry

    lax.fori_loop(0, tt // _RW, row_body, 0)
    o_ref[...] = x_ref[...] + _dot(ybuf[...], wout_ref[...])


def conv_prompt(v, zg, x, dw, db, lg, lb, wout_bf, tt=512):
    b, t, _ = v.shape
    return pl.pallas_call(
        functools.partial(_conv_prompt_body, tt=tt),
        grid=(b, t // tt),
        in_specs=[pl.BlockSpec((None, tt, D_CONV), lambda bi, i: (bi, i, 0)),
                  pl.BlockSpec((None, tt, D_CONV), lambda bi, i: (bi, i, 0)),
                  pl.BlockSpec((None, tt, D_MODEL), lambda bi, i: (bi, i, 0)),
                  pl.BlockSpec((CONV_WIDTH, D_CONV), lambda bi, i: (0, 0)),
                  pl.BlockSpec((1, D_CONV), lambda bi, i: (0, 0)),
                  pl.BlockSpec((1, D_CONV), lambda bi, i: (0, 0)),
                  pl.BlockSpec((1, D_CONV), lambda bi, i: (0, 0)),
                  pl.BlockSpec((D_CONV, D_MODEL), lambda bi, i: (0, 0))],
        out_specs=pl.BlockSpec((None, tt, D_MODEL), lambda bi, i: (bi, i, 0)),
        out_shape=jax.ShapeDtypeStruct((b, t, D_MODEL), F32),
        scratch_shapes=[pltpu.VMEM((tt + _HDR, D_CONV), F32),
                        pltpu.VMEM((tt, D_CONV), F32),
                        pltpu.VMEM((tt, D_CONV), BF16)],
        compiler_params=_params("arbitrary", "arbitrary"),
        name="conv_prompt",
    )(v, zg, x, dw, db, lg, lb, wout_bf)


def _conv_sample_body(st_ref, v_ref, zg_ref, x_ref, dwa_ref, db_ref, lg_ref, lb_ref, wout_ref,
                      o_ref, ybuf, *, bs):
    def seq_body(s, carry):
        r0 = pl.multiple_of(s * SPAD, SPAD)
        acc = jnp.zeros((SPAD, D_CONV), F32)
        for r in range(CONV_STATE):
            acc = acc + st_ref[s, pl.ds(r, 1), :] * dwa_ref[r]
        for u in range(SPAD):
            acc = acc + v_ref[pl.ds(r0 + u, 1), :] * dwa_ref[CONV_STATE + u]
        y = _ln_gate(acc + db_ref[...], lg_ref[...], lb_ref[...], zg_ref[pl.ds(r0, SPAD), :])
        ybuf[pl.ds(r0, SPAD), :] = y
        return carry

    lax.fori_loop(0, bs, seq_body, 0)
    o_ref[...] = x_ref[...] + _dot(ybuf[...].astype(BF16), wout_ref[...])


def conv_sample(state, v, zg, x, dwa, db, lg, lb, wout_bf, bs=16):
    nb = state.shape[0]
    rows = bs * SPAD
    return pl.pallas_call(
        functools.partial(_conv_sample_body, bs=bs),
        grid=(nb // bs,),
        in_specs=[pl.BlockSpec((bs, CONV_STATE, D_CONV), lambda i: (i, 0, 0)),
                  pl.BlockSpec((rows, D_CONV), lambda i: (i, 0)),
                  pl.BlockSpec((rows, D_CONV), lambda i: (i, 0)),
                  pl.BlockSpec((rows, D_MODEL), lambda i: (i, 0)),
                  pl.BlockSpec((CONV_STATE + SPAD, SPAD, D_CONV), lambda i: (0, 0, 0)),
                  pl.BlockSpec((1, D_CONV), lambda i: (0, 0)),
                  pl.BlockSpec((1, D_CONV), lambda i: (0, 0)),
                  pl.BlockSpec((1, D_CONV), lambda i: (0, 0)),
                  pl.BlockSpec((D_CONV, D_MODEL), lambda i: (0, 0))],
        out_specs=pl.BlockSpec((rows, D_MODEL), lambda i: (i, 0)),
        out_shape=jax.ShapeDtypeStruct((nb * SPAD, D_MODEL), F32),
        scratch_shapes=[pltpu.VMEM((rows, D_CONV), F32)],
        compiler_params=_params("arbitrary"),
        name="conv_sample",
    )(state, v, zg, x, dwa, db, lg, lb, wout_bf)


def _nsa_inproj_body(x_ref, g_ref, w_ref, q_ref, zg_ref, kvp_ref, kvw_ref, kvb_ref, gate_ref):
    h = _rms(x_ref[...], g_ref[...]).astype(BF16)
    ch = 512
    for c in range(2):
        q = _dot(h, w_ref[:, c * ch:(c + 1) * ch])
        q_ref[:, c * ch:(c + 1) * ch] = (q * (HEAD_DIM ** -0.5)).astype(BF16)
    for c in range(2):
        z = _dot(h, w_ref[:, D_MODEL + c * ch:D_MODEL + (c + 1) * ch])
        zg_ref[:, c * ch:(c + 1) * ch] = _silu(z)
    kv0 = _dot(h, w_ref[:, 2048:2560])
    kvp_ref[:, 0:512] = kv0
    kv1 = _dot(h, w_ref[:, 2560:3072])
    kvp_ref[:, 512:1024] = kv1
    kvb_ref[:, 0:512] = kv1.astype(BF16)
    kv2 = _dot(h, w_ref[:, 3072:3584])
    kvw_ref[...] = kv2
    kvb_ref[:, 512:1024] = kv2.astype(BF16)
    gate_ref[...] = jax.nn.sigmoid(_dot(h, w_ref[:, 3584:3712]))


def nsa_inproj(x, g, w_bf, tm=256):
    n = x.shape[0]
    row = lambda i: (i, 0)
    return pl.pallas_call(
        _nsa_inproj_body,
        grid=(n // tm,),
        in_specs=[pl.BlockSpec((tm, D_MODEL), row),
                  pl.BlockSpec((1, D_MODEL), lambda i: (0, 0)),
                  pl.BlockSpec((D_MODEL, NSA_IN_PAD), lambda i: (0, 0))],
        out_specs=[pl.BlockSpec((tm, D_MODEL), row), pl.BlockSpec((tm, D_MODEL), row),
                   pl.BlockSpec((tm, 1024), row), pl.BlockSpec((tm, 512), row),
                   pl.BlockSpec((tm, 1024), row), pl.BlockSpec((tm, LANES), row)],
        out_shape=[jax.ShapeDtypeStruct((n, D_MODEL), BF16),
                   jax.ShapeDtypeStruct((n, D_MODEL), F32),
                   jax.ShapeDtypeStruct((n, 1024), F32),
                   jax.ShapeDtypeStruct((n, 512), F32),
                   jax.ShapeDtypeStruct((n, 1024), BF16),
                   jax.ShapeDtypeStruct((n, LANES), F32)],
        compiler_params=_params("arbitrary"),
        name="nsa_inproj",
    )(x, g, w_bf)


def _cmp_const_body(pe_ref, w1_ref, o_ref):
    o_ref[...] = _dot(pe_ref[...], w1_ref[...])


def cmp_const(pe_pad_bf, w1_flat_bf):
    return pl.pallas_call(
        _cmp_const_body,
        grid=(2,),
        in_specs=[pl.BlockSpec((None, 8, 2048), lambda s: (s, 0, 0)),
                  pl.BlockSpec((None, 2048, CMP_HIDDEN), lambda s: (s, 0, 0))],
        out_specs=pl.BlockSpec((None, 8, CMP_HIDDEN), lambda s: (s, 0, 0)),
        out_shape=jax.ShapeDtypeStruct((2, 8, CMP_HIDDEN), F32),
        compiler_params=_params("arbitrary"),
        name="cmp_const",
    )(pe_pad_bf, w1_flat_bf)


def _compress_pair(load_x, w1q_ref, w2h_ref, const_row, n):
    accs = [jnp.zeros((n, 2 * CMP_HIDDEN), F32) for _ in range(2)]
    for qd in range(4):
        xs = [load_x(4 * qd + j) for j in range(4)]
        w = w1q_ref[qd]
        for hf in range(2):
            lhs = jnp.concatenate([x[:, hf * HEAD_DIM:(hf + 1) * HEAD_DIM] for x in xs], axis=1)
            accs[hf] = accs[hf] + _dot(lhs.astype(BF16), w)
    out = jnp.zeros((n, LANES), F32)
    for hf in range(2):
        first = accs[hf][:, :CMP_HIDDEN]
        second = pltpu.roll(accs[hf][:, CMP_HIDDEN:], n - 1, 0)
        hid = _silu(first + second + const_row)
        out = out + _dot(hid.astype(BF16), w2h_ref[hf])
    rows = lax.broadcasted_iota(I32, (n, LANES), 0)
    return jnp.where(rows < n - 1, out, 0.0)


def _compress_prompt_body(x_ref, w1q_ref, w2h_ref, c_ref, o_ref, *, n):
    load_x = lambda l: x_ref[pl.ds(l, n, stride=CMP_STRIDE), :]
    o_ref[...] = _compress_pair(load_x, w1q_ref, w2h_ref, c_ref[0:1, :], n).astype(BF16)


def compress_prompt(kvp, w1q, w2h, const):
    b, t, _ = kvp.shape
    n = t // CMP_STRIDE
    return pl.pallas_call(
        functools.partial(_compress_prompt_body, n=n),
        grid=(b, 2, 2),
        in_specs=[pl.BlockSpec((None, t, LANES), lambda bi, s, w: (bi, 0, 2 * s + w)),
                  pl.BlockSpec((None, 4, 256, 256), lambda bi, s, w: (s, 0, 0, 0)),
                  pl.BlockSpec((None, 2, CMP_HIDDEN, LANES), lambda bi, s, w: (s, 0, 0, 0)),
                  pl.BlockSpec((None, 8, CMP_HIDDEN), lambda bi, s, w: (s, 0, 0))],
        out_specs=pl.BlockSpec((None, None, n, LANES), lambda bi, s, w: (bi, s, 0, w)),
        out_shape=jax.ShapeDtypeStruct((b, 2, n, D_KV), BF16),
        compiler_params=_params("arbitrary", "arbitrary", "arbitrary"),
        name="compress_prompt",
    )(kvp, w1q, w2h, const)


def _slope(hd):
    return 2.0 ** (-(hd + 1) / 2.0)


def _online_update(hd, s, v_win, acc, mst, lst, nrep):
    m_old = mst[hd]
    m_new = jnp.maximum(m_old, jnp.max(s, axis=-1, keepdims=True))
    p = jnp.exp(s - _rep(m_new, nrep))
    alpha = jnp.exp(m_old - m_new)
    lst[hd] = alpha * lst[hd] + jnp.sum(p, axis=-1, keepdims=True)
    acc[hd] = alpha * acc[hd] + _dot(p.astype(BF16), v_win)
    mst[hd] = m_new


def _nsa_prompt_body(slope_ref, q_ref, kvb_ref, cmp_ref, gate_ref, zg_ref, x_ref, wout_ref, fg_ref,
                     o_ref, qaug, acc, res, mst, lst, osc, d0s, scs, negs, *, tq, final):
    i = pl.program_id(1)
    q0 = i * tq
    tk = tq
    nrep = tk // LANES
    lane = lax.broadcasted_iota(I32, (tq, LANES), 1)
    lo_half = lane < HEAD_DIM

    kid = lax.broadcasted_iota(I32, (tq, tk), 1)
    qid = lax.broadcasted_iota(I32, (tq, tk), 0)
    d0s[...] = (kid - qid).astype(F32)

    for hd in range(N_HEADS):
        g = hd // HEADS_PER_GROUP
        w = q_ref[:, (hd // 2) * LANES:(hd // 2 + 1) * LANES].astype(F32)
        if hd % 2 != g % 2:
            w = pltpu.roll(w, HEAD_DIM, 1)
        keep = lo_half if g % 2 == 0 else jnp.logical_not(lo_half)
        qaug[hd, :, 0:LANES] = jnp.where(keep, w, 0.0).astype(BF16)

    ncmp = cmp_ref.shape[1]
    cid = lax.broadcasted_iota(I32, (tq, ncmp), 1)
    dist_c = (q0 + lax.broadcasted_iota(I32, (tq, ncmp), 0)) - (CMP_STRIDE * cid + CMP_BLOCK - 1)
    mask_c = dist_c >= 0
    dist_cf = dist_c.astype(F32)
    nsb = 64
    jt = lax.broadcasted_iota(I32, (nsb, ncmp), 0)
    ct = lax.broadcasted_iota(I32, (nsb, ncmp), 1)
    ov_t = jnp.where(CMP_STRIDE * ct < SEL_BLOCK * (jt + 1),
                     jnp.where(CMP_STRIDE * ct + CMP_BLOCK - 1 >= SEL_BLOCK * jt, 1.0, 0.0),
                     0.0).astype(BF16)
    jq = lax.broadcasted_iota(I32, (nsb, tq), 0)
    cur = lax.shift_right_logical(q0 + lax.broadcasted_iota(I32, (nsb, tq), 1), 6)
    elig = jq <= cur
    forced = (jq == 0) | (jq == cur) | (jq == cur - 1)

    def cmp_group(g, carry):
        m_w = pl.multiple_of(lax.shift_right_logical(g, 1) * LANES, LANES)
        kc = cmp_ref[0, :, pl.ds(m_w, LANES)]
        vc = cmp_ref[1, :, pl.ds(m_w, LANES)]
        psum = jnp.zeros((tq, ncmp), F32)
        for h in range(HEADS_PER_GROUP):
            hd = g * HEADS_PER_GROUP + h
            s = _dot_nt(qaug[hd, :, 0:LANES], kc) - slope_ref[hd] * dist_cf
            s = jnp.where(mask_c, s, NEG)
            m = jnp.max(s, axis=-1, keepdims=True)
            e = jnp.where(mask_c, jnp.exp(s - m), 0.0)
            p = e / jnp.maximum(jnp.sum(e, axis=-1, keepdims=True), 1e-30)
            psum = psum + p
            acc[hd] = _dot(p.astype(BF16), vc)
        p_hi = psum.astype(BF16)
        p_lo = (psum - p_hi.astype(F32)).astype(BF16)
        imp_t = _dot_nt(ov_t, p_hi) + _dot_nt(ov_t, p_lo)
        sc = jnp.where(elig, jnp.where(forced, FORCE_SCORE, imp_t), NEG)
        scs[...] = sc

        def rank_body(ii, rank):
            row = scs[pl.ds(ii, 1), :]
            beats = jnp.where(row > sc, 1, jnp.where(row == sc, jnp.where(ii < jq, 1, 0), 0))
            return rank + beats

        rank = lax.fori_loop(0, nsb, rank_body, jnp.zeros((nsb, tq), I32))
        neg = jnp.where(elig, jnp.where(rank < N_SELECT, 0.0, 1.0), 1.0)
        negs[pl.ds(pl.multiple_of(g * nsb, nsb), nsb), :] = neg
        return carry

    lax.fori_loop(0, N_KV_GROUPS, cmp_group, 0)

    for hd in range(N_HEADS):
        res[hd] = gate_ref[:, 3 * hd:3 * hd + 1] * acc[hd]

    neg_q = negs[...].T
    for hd in range(N_HEADS):
        g = hd // HEADS_PER_GROUP
        w = neg_q[:, (g // 2) * LANES:(g // 2 + 1) * LANES]
        keep = lo_half if g % 2 == 0 else jnp.logical_not(lo_half)
        qaug[hd, :, LANES:2 * LANES] = jnp.where(keep, w, 0.0).astype(BF16)

    def reset_state():
        for hd in range(N_HEADS):
            mst[hd] = jnp.full((tq, LANES), NEG, F32)
            lst[hd] = jnp.zeros((tq, LANES), F32)
            acc[hd] = jnp.zeros((tq, LANES), F32)

    def finalize(branch):
        for hd in range(N_HEADS):
            f = gate_ref[:, 3 * hd + branch:3 * hd + branch + 1] / lst[hd]
            res[hd] = res[hd] + acc[hd] * f

    reset_state()
    kkey = lax.broadcasted_iota(I32, (tk, LANES), 0)
    klane = lax.broadcasted_iota(I32, (tk, LANES), 1)

    def sel_tile(g, kt, diag):
        k0 = pl.multiple_of(kt * tk, tk)
        m_w = pl.multiple_of(lax.shift_right_logical(g, 1) * LANES, LANES)
        half = jnp.bitwise_and(g, 1)
        k_win = kvb_ref[pl.ds(k0, tk), pl.ds(m_w, LANES)]
        v_win = kvb_ref[pl.ds(k0, tk), pl.ds(D_KV + m_w, LANES)]
        blk = lax.shift_right_logical(k0 + kkey, 6)
        onehot = jnp.where(klane - half * HEAD_DIM == blk, NEG, 0.0).astype(BF16)
        k_aug = jnp.concatenate([k_win, onehot], axis=1)
        shift = (k0 - q0).astype(F32)
        for h in range(HEADS_PER_GROUP):
            hd = g * HEADS_PER_GROUP + h
            s = _dot_nt(qaug[hd], k_aug) + slope_ref[hd] * (d0s[...] + shift)
            if diag:
                s = jnp.where(d0s[...] > 0.0, NEG, s)
            _online_update(hd, s, v_win, acc, mst, lst, nrep)

    def sel_group(g, carry):
        lax.fori_loop(0, i, lambda kt, c: (sel_tile(g, kt, False), c)[1], 0)
        sel_tile(g, i, True)
        return carry

    lax.fori_loop(0, N_KV_GROUPS, sel_group, 0)
    finalize(1)

    reset_state()

    def win_tile(g, kt, mode):
        k0 = pl.multiple_of(kt * tk, tk)
        m_w = pl.multiple_of(lax.shift_right_logical(g, 1) * LANES, LANES)
        k_win = kvb_ref[pl.ds(k0, tk), pl.ds(2 * D_KV + m_w, LANES)]
        v_win = kvb_ref[pl.ds(k0, tk), pl.ds(3 * D_KV + m_w, LANES)]
        shift = (k0 - q0).astype(F32)
        for h in range(HEADS_PER_GROUP):
            hd = g * HEADS_PER_GROUP + h
            s = _dot_nt(qaug[hd, :, 0:LANES], k_win) + slope_ref[hd] * (d0s[...] + shift)
            if mode == "diag":
                s = jnp.where(d0s[...] > 0.0, NEG, s)
            elif mode == "far":
                s = jnp.where(d0s[...] < 0.0, NEG, s)
            _online_update(hd, s, v_win, acc, mst, lst, nrep)

    nback = WINDOW // tk

    def win_group(g, carry):
        @pl.when(i >= nback)
        def _():
            win_tile(g, i - nback, "far")
        for back in range(nback - 1, 0, -1):
            @pl.when(i >= back)
            def _():
                win_tile(g, i - back, "mid")
        win_tile(g, i, "diag")
        return carry

    lax.fori_loop(0, N_KV_GROUPS, win_group, 0)
    finalize(2)

    for w in range(N_HEADS // 2):
        ga = (2 * w) // HEADS_PER_GROUP
        a = res[2 * w]
        b = res[2 * w + 1]
        if ga % 2 == 1:
            a = pltpu.roll(a, HEAD_DIM, 1)
        else:
            b = pltpu.roll(b, HEAD_DIM, 1)
        ow = jnp.where(lo_half, a, b)
        osc[:, w * LANES:(w + 1) * LANES] = (ow * zg_ref[:, w * LANES:(w + 1) * LANES]).astype(BF16)
    xn = x_ref[...] + _dot(osc[...], wout_ref[...])
    if final:
        xn = _rms(xn, fg_ref[...])
    o_ref[...] = xn


def nsa_prompt_attn(slopes, q, kvb, cmpkv, gates, zg, x, wout_bf, fg, *, final, tq=256):
    b, t, _ = q.shape
    ncmp = cmpkv.shape[2]
    blk = lambda bi, i, s: (bi, i, 0)
    grid_spec = pltpu.PrefetchScalarGridSpec(
        num_scalar_prefetch=1,
        grid=(b, t // tq),
        in_specs=[pl.BlockSpec((None, tq, D_MODEL), blk),
                  pl.BlockSpec((None, t, 4 * D_KV), lambda bi, i, s: (bi, 0, 0)),
                  pl.BlockSpec((None, 2, ncmp, D_KV), lambda bi, i, s: (bi, 0, 0, 0)),
                  pl.BlockSpec((None, tq, LANES), blk),
                  pl.BlockSpec((None, tq, D_MODEL), blk),
                  pl.BlockSpec((None, tq, D_MODEL), blk),
                  pl.BlockSpec((D_MODEL, D_MODEL), lambda bi, i, s: (0, 0)),
                  pl.BlockSpec((1, D_MODEL), lambda bi, i, s: (0, 0))],
        out_specs=pl.BlockSpec((None, tq, D_MODEL), blk),
        scratch_shapes=[pltpu.VMEM((N_HEADS, tq, 2 * LANES), BF16),
                        pltpu.VMEM((N_HEADS, tq, LANES), F32),
                        pltpu.VMEM((N_HEADS, tq, LANES), F32),
                        pltpu.VMEM((N_HEADS, tq, LANES), F32),
                        pltpu.VMEM((N_HEADS, tq, LANES), F32),
                        pltpu.VMEM((tq, D_MODEL), BF16),
                        pltpu.VMEM((tq, tq), F32),
                        pltpu.VMEM((64, tq), F32),
                        pltpu.VMEM((N_KV_GROUPS * 64, tq), F32)])
    return pl.pallas_call(
        functools.partial(_nsa_prompt_body, tq=tq, final=final),
        grid_spec=grid_spec,
        out_shape=jax.ShapeDtypeStruct((b, t, D_MODEL), F32),
        compiler_params=_params("arbitrary", "arbitrary"),
        name="nsa_prompt_attn",
    )(slopes, q, kvb, cmpkv, gates, zg, x, wout_bf, fg)


_NPAGE_ROWS = PAGE_SIZE


def _nsa_sample_body(pt_ref, q_ref, kvb_ref, gate_ref, zg_ref, win_ref, w1q_ref, w2h_ref, c_ref,
                     cache_ref, o_ref, pbuf, sem, cmpw, ksel, vsel, kwin, vwin, ebig, bias_c, bias_s,
                     bias_w,
                     *, layer, npages):
    b = pl.program_id(0)
    nb = pl.num_programs(0)
    past = npages * PAGE_SIZE
    nchunk = past // CMP_STRIDE
    nsel = ksel.shape[0]
    nwin = kwin.shape[0]
    rows = N_HEADS * SPAD

    def page_copy(seq, p, slot):
        return pltpu.make_async_copy(cache_ref.at[layer, pt_ref[seq, p]],
                                     pbuf.at[slot, pl.ds(p * PAGE_SIZE, PAGE_SIZE)],
                                     sem.at[slot])

    def start_seq(seq, slot):
        for p in range(npages):
            page_copy(seq, p, slot).start()

    slot = lax.rem(b, 2)

    @pl.when(b == 0)
    def _():
        start_seq(0, 0)
        r = lax.broadcasted_iota(I32, (rows, 1), 0)
        slope = jnp.exp2(-0.5 * (lax.shift_right_logical(r, 3) + 1).astype(F32))
        qpos = past + jnp.bitwise_and(r, SPAD - 1)
        c_end = CMP_STRIDE * lax.broadcasted_iota(I32, (rows, nchunk), 1) + CMP_BLOCK - 1
        ccol = lax.broadcasted_iota(I32, (rows, nchunk), 1)
        bias_c[...] = jnp.where(ccol < nchunk - 1, -slope * (qpos - c_end).astype(F32), NEG)
        kpos = lax.broadcasted_iota(I32, (rows, nsel), 1)
        bias_s[...] = jnp.where(kpos <= qpos, -slope * (qpos - kpos).astype(F32), NEG)
        wcol = lax.broadcasted_iota(I32, (rows, nwin), 1)
        wpos = jnp.where(wcol < WINDOW, past - WINDOW + wcol, past + wcol - WINDOW)
        dist_w = qpos - wpos
        ok = (dist_w >= 0) & (dist_w <= WINDOW) & (wcol < WINDOW + SPAD)
        bias_w[...] = jnp.where(ok, -slope * dist_w.astype(F32), NEG)
        ej = lax.broadcasted_iota(I32, (64, nsel), 0)
        ek = lax.shift_right_logical(lax.broadcasted_iota(I32, (64, nsel), 1), 6)
        ebig[...] = jnp.where(ej == ek, NEG, 0.0).astype(BF16)
        zpad = jnp.zeros((LANES, D_KV), BF16)
        ksel[past:past + LANES, :] = zpad
        vsel[past:past + LANES, :] = zpad
        kwin[WINDOW:WINDOW + LANES, :] = zpad
        vwin[WINDOW:WINDOW + LANES, :] = zpad

    @pl.when(b + 1 < nb)
    def _():
        start_seq(b + 1, 1 - slot)

    for p in range(npages):
        page_copy(b, p, slot).wait()

    buf = pbuf.at[slot]

    ksel[0:past, :] = buf[:, 2 * D_KV:3 * D_KV].astype(BF16)
    vsel[0:past, :] = buf[:, 3 * D_KV:4 * D_KV].astype(BF16)
    ksel[past:past + SPAD, :] = kvb_ref[:, 0:D_KV]
    vsel[past:past + SPAD, :] = kvb_ref[:, D_KV:2 * D_KV]
    kwin[0:WINDOW, :] = win_ref[:, 0:D_KV].astype(BF16)
    vwin[0:WINDOW, :] = win_ref[:, D_KV:2 * D_KV].astype(BF16)
    kwin[WINDOW:WINDOW + SPAD, :] = kvb_ref[:, 2 * D_KV:3 * D_KV]
    vwin[WINDOW:WINDOW + SPAD, :] = kvb_ref[:, 3 * D_KV:4 * D_KV]

    cmp_out = []
    for w in range(4):
        cmpw[w] = buf[:, w * LANES:(w + 1) * LANES]
        s = w // 2
        cmp_out.append(_compress_pair(
            lambda l, w=w: cmpw[w, pl.ds(l, nchunk, stride=CMP_STRIDE), :],
            w1q_ref.at[s], w2h_ref.at[s], c_ref[s, 0:1, :], nchunk))
    kc = jnp.concatenate(cmp_out[0:2], axis=1).astype(BF16)
    vc = jnp.concatenate(cmp_out[2:4], axis=1).astype(BF16)

    qf = q_ref[...].astype(F32)
    lane4 = lax.broadcasted_iota(I32, (SPAD, D_KV), 1)
    pieces = []
    for hd in range(N_HEADS):
        g, h = divmod(hd, HEADS_PER_GROUP)
        w = qf[:, g * D_KV:(g + 1) * D_KV]
        sh = ((g - h) * HEAD_DIM) % D_KV
        if sh:
            w = pltpu.roll(w, sh, 1)
        inside = (lane4 >= g * HEAD_DIM) & (lane4 < (g + 1) * HEAD_DIM)
        pieces.append(jnp.where(inside, w, 0.0))
    qbd = jnp.concatenate(pieces, axis=0).astype(BF16)

    s_c = _dot_nt(qbd, kc) + bias_c[...]
    m_c = jnp.max(s_c, axis=-1, keepdims=True)
    e_c = jnp.exp(s_c - m_c)
    p_c = e_c / jnp.maximum(jnp.sum(e_c, axis=-1, keepdims=True), 1e-30)
    o_c = _dot(p_c.astype(BF16), vc)

    psum = jnp.concatenate(
        [sum(p_c[(g * HEADS_PER_GROUP + h) * SPAD:(g * HEADS_PER_GROUP + h + 1) * SPAD]
             for h in range(HEADS_PER_GROUP)) for g in range(N_KV_GROUPS)], axis=0)
    cj = lax.broadcasted_iota(I32, (nchunk, 64), 0)
    jj = lax.broadcasted_iota(I32, (nchunk, 64), 1)
    ov = jnp.where(CMP_STRIDE * cj < SEL_BLOCK * (jj + 1),
                   jnp.where(CMP_STRIDE * cj + CMP_BLOCK - 1 >= SEL_BLOCK * jj, 1.0, 0.0),
                   0.0).astype(BF16)
    p_hi = psum.astype(BF16)
    p_lo = (psum - p_hi.astype(F32)).astype(BF16)
    imp = _dot(p_hi, ov) + _dot(p_lo, ov)
    nrow = N_KV_GROUPS * SPAD
    jb = lax.broadcasted_iota(I32, (nrow, 64), 1)
    cur = lax.shift_right_logical(
        past + jnp.bitwise_and(lax.broadcasted_iota(I32, (nrow, 64), 0), SPAD - 1), 6)
    elig = jb <= cur
    forced = (jb == 0) | (jb == cur) | (jb == cur - 1)
    sc = jnp.where(elig, jnp.where(forced, FORCE_SCORE, imp), NEG)
    rank = jnp.zeros((nrow, 64), I32)
    for ii in range(past // SEL_BLOCK + 1):
        col = sc[:, ii:ii + 1]
        rank = rank + jnp.where(col > sc, 1, jnp.where(col == sc, jnp.where(ii < jb, 1, 0), 0))
    neg = jnp.where(elig, jnp.where(rank < N_SELECT, 0.0, 1.0), 1.0)
    neg_rows = jnp.concatenate(
        [neg[(hd // HEADS_PER_GROUP) * SPAD:(hd // HEADS_PER_GROUP + 1) * SPAD] for hd in range(N_HEADS)],
        axis=0).astype(BF16)

    s_s = _dot_nt(qbd, ksel[...]) + _dot(neg_rows, ebig[...]) + bias_s[...]
    m_s = jnp.max(s_s, axis=-1, keepdims=True)
    e_s = jnp.exp(s_s - m_s)
    o_s = _dot(e_s.astype(BF16), vsel[...]) / jnp.sum(e_s, axis=-1, keepdims=True)

    s_w = _dot_nt(qbd, kwin[...]) + bias_w[...]
    m_w = jnp.max(s_w, axis=-1, keepdims=True)
    e_w = jnp.exp(s_w - m_w)
    o_w = _dot(e_w.astype(BF16), vwin[...]) / jnp.sum(e_w, axis=-1, keepdims=True)

    gates = gate_ref[...]
    gcol = lambda br: jnp.concatenate(
        [gates[:, 3 * hd + br:3 * hd + br + 1] for hd in range(N_HEADS)], axis=0)
    o_all = gcol(0) * o_c + gcol(1) * o_s + gcol(2) * o_w

    outs = []
    for g in range(N_KV_GROUPS):
        inside = (lane4 >= g * HEAD_DIM) & (lane4 < (g + 1) * HEAD_DIM)
        tot = jnp.zeros((SPAD, D_KV), F32)
        for h in range(HEADS_PER_GROUP):
            hd = g * HEADS_PER_GROUP + h
            piece = jnp.where(inside, o_all[hd * SPAD:(hd + 1) * SPAD], 0.0)
            sh = ((h - g) * HEAD_DIM) % D_KV
            if sh:
                piece = pltpu.roll(piece, sh, 1)
            tot = tot + piece
        outs.append(tot)
    o_ref[...] = jnp.concatenate(outs, axis=1) * zg_ref[...]


def nsa_sample_attn(page_table, q, kvb, gates, zg, win, w1q, w2h, const, cache, *, layer):
    nb, npages = page_table.shape
    past = npages * PAGE_SIZE
    nchunk = past // CMP_STRIDE
    rows = N_HEADS * SPAD
    seq = lambda i, pt: (i, 0)
    grid_spec = pltpu.PrefetchScalarGridSpec(
        num_scalar_prefetch=1,
        grid=(nb,),
        in_specs=[pl.BlockSpec((SPAD, D_MODEL), seq),
                  pl.BlockSpec((SPAD, 4 * D_KV), seq),
                  pl.BlockSpec((SPAD, LANES), seq),
                  pl.BlockSpec((SPAD, D_MODEL), seq),
                  pl.BlockSpec((None, WINDOW, 2 * D_KV), lambda i, pt: (i, 0, 0)),
                  pl.BlockSpec((2, 4, 256, 256), lambda i, pt: (0, 0, 0, 0)),
                  pl.BlockSpec((2, 2, CMP_HIDDEN, LANES), lambda i, pt: (0, 0, 0, 0)),
                  pl.BlockSpec((2, 8, CMP_HIDDEN), lambda i, pt: (0, 0, 0)),
                  pl.BlockSpec(memory_space=pl.ANY)],
        out_specs=pl.BlockSpec((SPAD, D_MODEL), seq),
        scratch_shapes=[pltpu.VMEM((2, past, 4 * D_KV), F32),
                        pltpu.SemaphoreType.DMA((2,)),
                        pltpu.VMEM((4, past, LANES), F32),
                        pltpu.VMEM((past + LANES, D_KV), BF16),
                        pltpu.VMEM((past + LANES, D_KV), BF16),
                        pltpu.VMEM((WINDOW + LANES, D_KV), BF16),
                        pltpu.VMEM((WINDOW + LANES, D_KV), BF16),
                        pltpu.VMEM((64, past + LANES), BF16),
                        pltpu.VMEM((rows, nchunk), F32),
                        pltpu.VMEM((rows, past + LANES), F32),
                        pltpu.VMEM((rows, WINDOW + LANES), F32)])
    return pl.pallas_call(
        functools.partial(_nsa_sample_body, layer=layer, npages=npages),
        grid_spec=grid_spec,
        out_shape=jax.ShapeDtypeStruct((nb * SPAD, D_MODEL), F32),
        compiler_params=_params("arbitrary"),
        name="nsa_sample_attn",
    )(page_table, q, kvb, gates, zg, win, w1q, w2h, const, cache)


def _outproj_body(o_ref, x_ref, w_ref, fg_ref, y_ref, *, final):
    xn = x_ref[...] + _dot(o_ref[...].astype(BF16), w_ref[...])
    if final:
        xn = _rms(xn, fg_ref[...])
    y_ref[...] = xn


def outproj(o, x, w_bf, fg, *, final, tm=256):
    n = o.shape[0]
    row = lambda i: (i, 0)
    return pl.pallas_call(
        functools.partial(_outproj_body, final=final),
        grid=(n // tm,),
        in_specs=[pl.BlockSpec((tm, D_MODEL), row), pl.BlockSpec((tm, D_MODEL), row),
                  pl.BlockSpec((D_MODEL, D_MODEL), lambda i: (0, 0)),
                  pl.BlockSpec((1, D_MODEL), lambda i: (0, 0))],
        out_specs=pl.BlockSpec((tm, D_MODEL), row),
        out_shape=jax.ShapeDtypeStruct((n, D_MODEL), F32),
        compiler_params=_params("arbitrary"),
        name="outproj",
    )(o, x, w_bf, fg)


def _prep_cmp_weights(pe, w1, w2):
    top = w1[:, :CMP_STRIDE].reshape(2, 4, 4 * HEAD_DIM, CMP_HIDDEN)
    bot = w1[:, CMP_STRIDE:].reshape(2, 4, 4 * HEAD_DIM, CMP_HIDDEN)
    w1q = jnp.concatenate([top, bot], axis=-1).astype(BF16)
    w2h = jnp.zeros((2, 2, CMP_HIDDEN, LANES), F32)
    for hf in range(2):
        w2h = w2h.at[:, hf, :, hf * HEAD_DIM:(hf + 1) * HEAD_DIM].set(w2)
    pe_pad = jnp.zeros((2, 8, CMP_BLOCK * HEAD_DIM), F32).at[:, 0].set(pe.reshape(2, -1))
    w1_flat = w1.reshape(2, CMP_BLOCK * HEAD_DIM, CMP_HIDDEN)
    return w1q, w2h.astype(BF16), pe_pad.astype(BF16), w1_flat.astype(BF16)


def _prep_conv_sample_taps(dw):
    r = jnp.arange(CONV_STATE + SPAD)[:, None]
    t = jnp.arange(SPAD)[None, :]
    k = r - t
    valid = (k >= 0) & (k < CONV_WIDTH)
    return jnp.where(valid[..., None], dw[jnp.clip(k, 0, CONV_WIDTH - 1)], 0.0)


def kernel(x_prompt, x_sample, cache_kv, state_kv_win, state_conv, page_table, norm_g, final_norm_g,
           conv_w_in, conv_dw_w, conv_dw_b, conv_ln_g, conv_ln_b, conv_w_out,
           nsa_w_in, nsa_w_out, nsa_cmp_pe, nsa_cmp_w1, nsa_cmp_w2):
    bsz, seq, _ = x_prompt.shape
    nb = x_sample.shape[0]
    n_layers_nsa = nsa_w_in.shape[0]
    n_pool = cache_kv.shape[1]
    win_keep = state_kv_win.shape[2]
    xp = x_prompt
    xs = jnp.pad(x_sample, ((0, 0), (0, SPAD - DEC_SEQ), (0, 0))).reshape(nb * SPAD, D_MODEL)
    cache = cache_kv.reshape(n_layers_nsa, n_pool, PAGE_SIZE, 4 * D_KV)
    slopes = jnp.asarray([_slope(hd) for hd in range(N_HEADS)], F32)
    fg = final_norm_g.reshape(1, D_MODEL)

    kv_p, kv_s, win_p, win_s, conv_p, conv_s = [], [], [], [], [], []
    for layer in range(DEPTH):
        j = layer // 2
        g = norm_g[layer].reshape(1, D_MODEL)
        last = layer == DEPTH - 1
        if layer % 2 == 0:
            w_in = conv_w_in[j].astype(BF16)
            w_out = conv_w_out[j].astype(BF16)
            db = conv_dw_b[j].reshape(1, D_CONV)
            lg = conv_ln_g[j].reshape(1, D_CONV)
            lb = conv_ln_b[j].reshape(1, D_CONV)
            v, zg = conv_inproj(xp.reshape(bsz * seq, D_MODEL), g, w_in)
            v = v.reshape(bsz, seq, D_CONV)
            xp = conv_prompt(v, zg.reshape(bsz, seq, D_CONV), xp, conv_dw_w[j], db, lg, lb, w_out)
            conv_p.append(v[:, seq - CONV_STATE:])
            vs, zgs = conv_inproj(xs, g, w_in)
            xs = conv_sample(state_conv[j], vs, zgs, xs, _prep_conv_sample_taps(conv_dw_w[j]),
                             db, lg, lb, w_out)
            vs3 = vs.reshape(nb, SPAD, D_CONV)[:, :DEC_SEQ]
            conv_s.append(jnp.concatenate([state_conv[j][:, DEC_SEQ:], vs3], axis=1))
        else:
            w_in = jnp.pad(nsa_w_in[j], ((0, 0), (0, NSA_IN_PAD - NSA_IN))).astype(BF16)
            w_out = nsa_w_out[j].astype(BF16)
            w1q, w2h, pe_pad, w1_flat = _prep_cmp_weights(nsa_cmp_pe[j], nsa_cmp_w1[j], nsa_cmp_w2[j])
            const = cmp_const(pe_pad, w1_flat)
            q, zg, kvp, kvw, kvb, gates = nsa_inproj(xp.reshape(bsz * seq, D_MODEL), g, w_in)
            kvp3 = kvp.reshape(bsz, seq, 4 * D_KV)
            cmpkv = compress_prompt(kvp3, w1q, w2h, const)
            xp = nsa_prompt_attn(slopes, q.reshape(bsz, seq, D_MODEL), kvb.reshape(bsz, seq, 4 * D_KV),
                                 cmpkv, gates.reshape(bsz, seq, LANES), zg.reshape(bsz, seq, D_MODEL),
                                 xp, w_out, fg, final=last)
            kv_p.append(kvp3.reshape(bsz, seq, 4, N_KV_GROUPS, HEAD_DIM))
            win_p.append(kvw.reshape(bsz, seq, 2, N_KV_GROUPS, HEAD_DIM)[:, seq - min(WINDOW, seq):])
            qs, zgs, kvps, kvws, kvbs, gs = nsa_inproj(xs, g, w_in)
            win = state_kv_win[j].reshape(nb, win_keep, 2 * D_KV)
            og = nsa_sample_attn(page_table, qs, kvbs, gs, zgs, win, w1q, w2h, const, cache, layer=j)
            xs = outproj(og, xs, w_out, fg, final=last)
            kv_s.append(kvps.reshape(nb, SPAD, 4, N_KV_GROUPS, HEAD_DIM)[:, :DEC_SEQ])
            new_w = kvws.reshape(nb, SPAD, 2, N_KV_GROUPS, HEAD_DIM)[:, :DEC_SEQ]
            win_s.append(jnp.concatenate([state_kv_win[j], new_w], axis=1)[:, -win_keep:])
    y_sample = xs.reshape(nb, SPAD, D_MODEL)[:, :DEC_SEQ]
    return (xp, y_sample, jnp.stack(kv_p), jnp.stack(kv_s), jnp.stack(win_p), jnp.stack(win_s),
            jnp.stack(conv_p), jnp.stack(conv_s))
```

```python
import functools
import math

import jax
import jax.numpy as jnp
import ml_dtypes
import numpy as np
from jax import lax
from jax.experimental import pallas as pl
from jax.experimental.pallas import tpu as pltpu

F32 = jnp.float32
BF16 = jnp.bfloat16
I32 = jnp.int32

D_MODEL = 1024
DEPTH = 4
DEC_SEQ = 4
PAGE_SIZE = 128
D_CONV = 2048
CONV_WIDTH = 31
CONV_STATE = CONV_WIDTH - 1
N_HEADS = 16
HEAD_DIM = 64
N_KV_GROUPS = 4
HEADS_PER_GROUP = 4
D_KV = N_KV_GROUPS * HEAD_DIM
CMP_BLOCK = 32
CMP_STRIDE = 16
CMP_HIDDEN = 128
SEL_BLOCK = 64
N_SELECT = 16
WINDOW = 512
N_BRANCH = 3
NORM_EPS = 1e-6
FORCE_SCORE = 1e4
NEG = -1e30
LOG2E = math.log2(math.e)
SPAD = 8
LANES = 128
VMEM_LIMIT = 56 * 1024 * 1024


def _params(*sem):
    return pltpu.CompilerParams(dimension_semantics=sem, vmem_limit_bytes=VMEM_LIMIT)


def _rms(x, g):
    return x * lax.rsqrt(jnp.mean(x * x, axis=-1, keepdims=True) + NORM_EPS) * g


def _silu(x):
    return x * jax.nn.sigmoid(x)


def _dot(a, b):
    return jnp.dot(a, b, preferred_element_type=F32)


def _dot_nt(a, b):
    return lax.dot_general(a, b, (((1,), (1,)), ((), ())), preferred_element_type=F32)


def _rep(x, n):
    return jnp.concatenate([x] * n, axis=1) if n > 1 else x


def _slope2(hd):
    return 2.0 ** (-(hd + 1) / 2.0) * LOG2E


def _hi_lo(x):
    hi = float(np.float32(x).astype(ml_dtypes.bfloat16).astype(np.float32))
    lo = float(np.float32(x - hi).astype(ml_dtypes.bfloat16).astype(np.float32))
    return hi, lo


def _conv_inproj_body(x_ref, g_ref, w_ref, v_ref, zg_ref):
    h = _rms(x_ref[...], g_ref[...]).astype(BF16)
    ch = 512
    for c in range(D_CONV // ch):
        lo = c * ch
        a = _dot(h, w_ref[:, lo:lo + ch])
        gl = _dot(h, w_ref[:, D_CONV + lo:D_CONV + lo + ch])
        z = _dot(h, w_ref[:, 2 * D_CONV + lo:2 * D_CONV + lo + ch])
        v_ref[:, lo:lo + ch] = a * jax.nn.sigmoid(gl)
        zg_ref[:, lo:lo + ch] = _silu(z)


def conv_inproj(x, g, w_bf, tm=256):
    n = x.shape[0]
    return pl.pallas_call(
        _conv_inproj_body,
        grid=(n // tm,),
        in_specs=[pl.BlockSpec((tm, D_MODEL), lambda i: (i, 0)),
                  pl.BlockSpec((1, D_MODEL), lambda i: (0, 0)),
                  pl.BlockSpec((D_MODEL, 3 * D_CONV), lambda i: (0, 0))],
        out_specs=[pl.BlockSpec((tm, D_CONV), lambda i: (i, 0)),
                   pl.BlockSpec((tm, D_CONV), lambda i: (i, 0))],
        out_shape=[jax.ShapeDtypeStruct((n, D_CONV), F32),
                   jax.ShapeDtypeStruct((n, D_CONV), F32)],
        compiler_params=_params("arbitrary"),
        name="conv_inproj",
    )(x, g, w_bf)


_HDR = 32
_RW = 32
_PH = 4
_NSLAB = D_CONV // LANES
_GROUPS_PER_BODY = 2


def _ln_gate(c, lg, lb, zg):
    mu = jnp.mean(c, axis=-1, keepdims=True)
    xc = c - mu
    var = jnp.mean(xc * xc, axis=-1, keepdims=True)
    y = xc * lax.rsqrt(var + NORM_EPS) * lg + lb
    return _silu(y) * zg


def _conv_prompt_body(v_ref, zg_ref, x_ref, dw_ref, db_ref, lg_ref, lb_ref, wout_ref, o_ref,
                      vbuf, cbuf, ybuf, *, tt):
    i = pl.program_id(1)

    @pl.when(i == 0)
    def _():
        vbuf[:, 0:_HDR, :] = jnp.zeros((_NSLAB, _HDR, LANES), F32)

    @pl.when(i > 0)
    def _():
        vbuf[:, 0:_HDR, :] = vbuf[:, tt:tt + _HDR, :]

    for w in range(_NSLAB):
        vbuf[w, _HDR:_HDR + tt, :] = v_ref[:, w * LANES:(w + 1) * LANES]
    off = _HDR - CONV_STATE

    def row_body(r, carry):
        r0 = pl.multiple_of(r * _RW, _RW)

        def slab_body(w, carry2):
            c0 = pl.multiple_of(w * LANES, LANES)
            accs = [jnp.zeros((8, LANES), F32) for _ in range(_PH)]
            for j in range(CONV_WIDTH):
                wj = dw_ref[pl.ds(j, 1), pl.ds(c0, LANES)]
                for u in range(_PH):
                    tap = vbuf[w, pl.ds(r0 + u + j + off, 8, stride=_PH), :]
                    accs[u] = accs[u] + tap * wj
            bias = db_ref[:, pl.ds(c0, LANES)]
            for u in range(_PH):
                cbuf[w, pl.ds(r0 + u, 8, stride=_PH), :] = accs[u] + bias
            return carry2

        lax.fori_loop(0, _NSLAB, slab_body, 0)
        c = jnp.concatenate([cbuf[w, pl.ds(r0, _RW), :] for w in range(_NSLAB)], axis=1)
        y = _ln_gate(c, lg_ref[...], lb_ref[...], zg_ref[pl.ds(r0, _RW), :])
        ybuf[pl.ds(r0, _RW), :] = y.astype(BF16)
        return carry

    lax.fori_loop(0, tt // _RW, row_body, 0)
    o_ref[...] = x_ref[...] + _dot(ybuf[...], wout_ref[...])


def conv_prompt(v, zg, x, dw, db, lg, lb, wout_bf, tt=512):
    b, t, _ = v.shape
    return pl.pallas_call(
        functools.partial(_conv_prompt_body, tt=tt),
        grid=(b, t // tt),
        in_specs=[pl.BlockSpec((None, tt, D_CONV), lambda bi, i: (bi, i, 0)),
                  pl.BlockSpec((None, tt, D_CONV), lambda bi, i: (bi, i, 0)),
                  pl.BlockSpec((None, tt, D_MODEL), lambda bi, i: (bi, i, 0)),
                  pl.BlockSpec((CONV_WIDTH, D_CONV), lambda bi, i: (0, 0)),
                  pl.BlockSpec((1, D_CONV), lambda bi, i: (0, 0)),
                  pl.BlockSpec((1, D_CONV), lambda bi, i: (0, 0)),
                  pl.BlockSpec((1, D_CONV), lambda bi, i: (0, 0)),
                  pl.BlockSpec((D_CONV, D_MODEL), lambda bi, i: (0, 0))],
        out_specs=pl.BlockSpec((None, tt, D_MODEL), lambda bi, i: (bi, i, 0)),
        out_shape=jax.ShapeDtypeStruct((b, t, D_MODEL), F32),
        scratch_shapes=[pltpu.VMEM((_NSLAB, tt + _HDR, LANES), F32),
                        pltpu.VMEM((_NSLAB, tt, LANES), F32),
                        pltpu.VMEM((tt, D_CONV), BF16)],
        compiler_params=_params("arbitrary", "arbitrary"),
        name="conv_prompt",
    )(v, zg, x, dw, db, lg, lb, wout_bf)


def _conv_sample_body(st_ref, v_ref, zg_ref, x_ref, dw_ref, db_ref, lg_ref, lb_ref, wout_ref,
                      o_ref, ybuf, *, bs):
    ch = 512
    for t in range(DEC_SEQ):
        parts = []
        for c in range(D_CONV // ch):
            cs = slice(c * ch, (c + 1) * ch)
            acc = jnp.zeros((bs, ch), F32)
            for j in range(CONV_WIDTH):
                r = t + j
                src = st_ref[r, :, cs] if r < CONV_STATE else v_ref[r - CONV_STATE, :, cs]
                acc = acc + src * dw_ref[j:j + 1, cs]
            parts.append(acc)
        c_t = jnp.concatenate(parts, axis=1) + db_ref[...]
        ybuf[t * bs:(t + 1) * bs, :] = _ln_gate(c_t, lg_ref[...], lb_ref[...], zg_ref[t])
    y = _dot(ybuf[...].astype(BF16), wout_ref[...])
    for t in range(DEC_SEQ):
        o_ref[t] = x_ref[t] + y[t * bs:(t + 1) * bs]
    for t in range(DEC_SEQ, SPAD):
        o_ref[t] = jnp.zeros((bs, D_MODEL), F32)


def conv_sample(state_t, v, zg, x, dw, db, lg, lb, wout_bf, *, layer, bs=16):
    nb = state_t.shape[2]
    blk = lambda i: (0, i, 0)
    return pl.pallas_call(
        functools.partial(_conv_sample_body, bs=bs),
        grid=(nb // bs,),
        in_specs=[pl.BlockSpec((None, CONV_STATE, bs, D_CONV), lambda i: (layer, 0, i, 0)),
                  pl.BlockSpec((DEC_SEQ, bs, D_CONV), blk),
                  pl.BlockSpec((DEC_SEQ, bs, D_CONV), blk),
                  pl.BlockSpec((SPAD, bs, D_MODEL), blk),
                  pl.BlockSpec((CONV_WIDTH, D_CONV), lambda i: (0, 0)),
                  pl.BlockSpec((1, D_CONV), lambda i: (0, 0)),
                  pl.BlockSpec((1, D_CONV), lambda i: (0, 0)),
                  pl.BlockSpec((1, D_CONV), lambda i: (0, 0)),
                  pl.BlockSpec((D_CONV, D_MODEL), lambda i: (0, 0))],
        out_specs=pl.BlockSpec((SPAD, bs, D_MODEL), blk),
        out_shape=jax.ShapeDtypeStruct((SPAD, nb, D_MODEL), F32),
        scratch_shapes=[pltpu.VMEM((DEC_SEQ * bs, D_CONV), F32)],
        compiler_params=_params("arbitrary"),
        name="conv_sample",
    )(state_t, v, zg, x, dw, db, lg, lb, wout_bf)


def _nsa_inproj_body(*refs, sample):
    if sample:
        x_ref, g_ref, wa_ref, wt_ref, wn_ref, q_ref, zg_ref, gate_ref, kvp_ref, kvw_ref, kvn_ref = refs
    else:
        x_ref, g_ref, wa_ref, wt_ref, q_ref, zg_ref, gate_ref, kvp_ref, kvw_ref, kcn_ref, kvtb_ref = refs
    h = _rms(x_ref[...], g_ref[...]).astype(BF16)
    ch = 512
    for c in range(2):
        q = _dot(h, wa_ref[:, c * ch:(c + 1) * ch])
        q_ref[:, c * ch:(c + 1) * ch] = (q * (HEAD_DIM ** -0.5 * LOG2E)).astype(BF16)
    for c in range(2):
        z = _dot(h, wa_ref[:, D_MODEL + c * ch:D_MODEL + (c + 1) * ch])
        zg_ref[:, c * ch:(c + 1) * ch] = _silu(z)
    gate_ref[...] = jax.nn.sigmoid(_dot(h, wa_ref[:, 2560:2688]))
    for c in range(3):
        kvt = _dot_nt(wt_ref[c * ch:(c + 1) * ch, :], h)
        if c < 2:
            kvp_ref[c * ch:(c + 1) * ch, :] = kvt
        else:
            kvw_ref[...] = kvt
        if not sample and c >= 1:
            kvtb_ref[(c - 1) * ch:c * ch, :] = kvt.astype(BF16)
    if sample:
        for c in range(2):
            kvn_ref[:, c * ch:(c + 1) * ch] = _dot(h, wn_ref[:, c * ch:(c + 1) * ch]).astype(BF16)
    else:
        kcn_ref[...] = _dot(h, wa_ref[:, 2048:2560])


def nsa_inproj(x, g, w_a, w_t, w_n=None, tm=256):
    b, t, _ = x.shape
    sample = w_n is not None
    tok = lambda bi, i: (bi, i, 0)
    const = lambda bi, i: (0, 0)
    in_specs = [pl.BlockSpec((None, tm, D_MODEL), tok),
                pl.BlockSpec((1, D_MODEL), const),
                pl.BlockSpec((D_MODEL, 2688), const),
                pl.BlockSpec((6 * D_KV, D_MODEL), const)]
    out_specs = [pl.BlockSpec((None, tm, D_MODEL), tok),
                 pl.BlockSpec((None, tm, D_MODEL), tok),
                 pl.BlockSpec((None, tm, LANES), tok),
                 pl.BlockSpec((None, 4 * D_KV, tm), lambda bi, i: (bi, 0, i)),
                 pl.BlockSpec((None, 2 * D_KV, tm), lambda bi, i: (bi, 0, i))]
    out_shape = [jax.ShapeDtypeStruct((b, t, D_MODEL), BF16),
                 jax.ShapeDtypeStruct((b, t, D_MODEL), F32),
                 jax.ShapeDtypeStruct((b, t, LANES), F32),
                 jax.ShapeDtypeStruct((b, 4 * D_KV, t), F32),
                 jax.ShapeDtypeStruct((b, 2 * D_KV, t), F32)]
    args = [x, g, w_a, w_t]
    if sample:
        in_specs.append(pl.BlockSpec((D_MODEL, 4 * D_KV), const))
        out_specs.append(pl.BlockSpec((None, tm, 4 * D_KV), tok))
        out_shape.append(jax.ShapeDtypeStruct((b, t, 4 * D_KV), BF16))
        args.append(w_n)
    else:
        out_specs.append(pl.BlockSpec((None, tm, 2 * D_KV), tok))
        out_shape.append(jax.ShapeDtypeStruct((b, t, 2 * D_KV), F32))
        out_specs.append(pl.BlockSpec((None, 4 * D_KV, tm), lambda bi, i: (bi, 0, i)))
        out_shape.append(jax.ShapeDtypeStruct((b, 4 * D_KV, t), BF16))
    return pl.pallas_call(
        functools.partial(_nsa_inproj_body, sample=sample),
        grid=(b, t // tm),
        in_specs=in_specs, out_specs=out_specs, out_shape=out_shape,
        compiler_params=_params("arbitrary", "arbitrary"),
        name="nsa_inproj_sample" if sample else "nsa_inproj",
    )(*args)


def _cmp_const_body(pe_ref, w1_ref, o_ref):
    o_ref[...] = _dot(pe_ref[...], w1_ref[...])


def cmp_const(pe_pad_bf, w1_flat_bf):
    return pl.pallas_call(
        _cmp_const_body,
        grid=(2,),
        in_specs=[pl.BlockSpec((None, 8, 2048), lambda s: (s, 0, 0)),
                  pl.BlockSpec((None, 2048, CMP_HIDDEN), lambda s: (s, 0, 0))],
        out_specs=pl.BlockSpec((None, 8, CMP_HIDDEN), lambda s: (s, 0, 0)),
        out_shape=jax.ShapeDtypeStruct((2, 8, CMP_HIDDEN), F32),
        compiler_params=_params("arbitrary"),
        name="cmp_const",
    )(pe_pad_bf, w1_flat_bf)


def _compress_hidden(load_x, w1q_ref, const_row, n):
    accs = [jnp.zeros((n, 2 * CMP_HIDDEN), F32) for _ in range(2)]
    for qd in range(4):
        xs = [load_x(4 * qd + j) for j in range(4)]
        w = w1q_ref[qd]
        for hf in range(2):
            lhs = jnp.concatenate([x[:, hf * HEAD_DIM:(hf + 1) * HEAD_DIM] for x in xs], axis=1)
            accs[hf] = accs[hf] + _dot(lhs.astype(BF16), w)
    hids = []
    for hf in range(2):
        first = accs[hf][:, :CMP_HIDDEN]
        second = pltpu.roll(accs[hf][:, CMP_HIDDEN:], n - 1, 0)
        hids.append(_silu(first + second + const_row).astype(BF16))
    return hids


def _compress_prompt_body(x_ref, w1q_ref, w2t_ref, c_ref, o_ref, *, n):
    load_x = lambda l: x_ref[pl.ds(l, n, stride=CMP_STRIDE), :]
    hids = _compress_hidden(load_x, w1q_ref, c_ref[0:1, :], n)
    out_t = _dot_nt(w2t_ref[0], hids[0]) + _dot_nt(w2t_ref[1], hids[1])
    cols = lax.broadcasted_iota(I32, (LANES, n), 1)
    o_ref[...] = jnp.where(cols < n - 1, out_t, 0.0).astype(BF16)


def compress_prompt(kcn, w1q, w2t, const):
    b, t, _ = kcn.shape
    n = t // CMP_STRIDE
    return pl.pallas_call(
        functools.partial(_compress_prompt_body, n=n),
        grid=(b, 2, 2),
        in_specs=[pl.BlockSpec((None, t, LANES), lambda bi, s, w: (bi, 0, 2 * s + w)),
                  pl.BlockSpec((None, 4, 256, 256), lambda bi, s, w: (s, 0, 0, 0)),
                  pl.BlockSpec((None, 2, LANES, CMP_HIDDEN), lambda bi, s, w: (s, 0, 0, 0)),
                  pl.BlockSpec((None, 8, CMP_HIDDEN), lambda bi, s, w: (s, 0, 0))],
        out_specs=pl.BlockSpec((None, None, LANES, n), lambda bi, s, w: (bi, s, w, 0)),
        out_shape=jax.ShapeDtypeStruct((b, 2, D_KV, n), BF16),
        compiler_params=_params("arbitrary", "arbitrary", "arbitrary"),
        name="compress_prompt",
    )(kcn, w1q, w2t, const)


def _bias_rows(pos_lo, pos_hi, nkeys):
    ri = lax.broadcasted_iota(I32, (LANES, nkeys), 0)
    return jnp.where(ri < 2, pos_lo, jnp.where(ri < 4, pos_hi, 0.0)).astype(BF16)


def _online_update(hd, s, v_aug, acc, mst, nrep):
    m_old = mst[hd]
    m_new = jnp.maximum(m_old, jnp.max(s, axis=-1, keepdims=True))
    p = jnp.exp2(s - _rep(m_new, nrep))
    alpha = jnp.exp2(m_old - m_new)
    acc[hd] = alpha * acc[hd] + _dot_nt(p.astype(BF16), v_aug)
    mst[hd] = m_new


def _nsa_prompt_body(q_ref, kvt_ref, cmp_ref, gate_ref, zg_ref, x_ref, wout_ref, fg_ref,
                     o_ref, qaug, acc, res, mst, osc, scs, negs, ones_blk, *, tq, final):
    i = pl.program_id(1)
    q0 = i * tq
    tk = tq
    nrep = tk // LANES
    lane = lax.broadcasted_iota(I32, (tq, LANES), 1)
    lo_half = lane < HEAD_DIM
    kid = lax.broadcasted_iota(I32, (tq, tk), 1)
    qid = lax.broadcasted_iota(I32, (tq, tk), 0)
    above = kid > qid
    below = kid < qid

    orow = lax.broadcasted_iota(I32, (HEAD_DIM, tk), 0)
    ones_blk[...] = jnp.where(orow == 0, 1.0, 0.0).astype(BF16)
    zeros_blk = jnp.zeros((HEAD_DIM, tk), BF16)

    for hd in range(N_HEADS):
        w = q_ref[:, (hd // 2) * LANES:(hd // 2 + 1) * LANES].astype(F32)
        if hd % 2 == 1:
            w = pltpu.roll(w, HEAD_DIM, 1)
        qaug[hd, :, 0:LANES] = jnp.where(lo_half, w, 0.0).astype(BF16)
        hi, lo = _hi_lo(_slope2(hd))
        cols = jnp.where(lane == 0, hi, jnp.where(lane == 1, lo, jnp.where(
            lane == 2, 256.0 * hi, jnp.where(lane == 3, 256.0 * lo, 0.0))))
        qaug[hd, :, LANES:2 * LANES] = cols.astype(BF16)

    ncmp = cmp_ref.shape[2]
    cid = lax.broadcasted_iota(I32, (tq, ncmp), 1)
    c_end = CMP_STRIDE * cid + CMP_BLOCK - 1
    mask_c = c_end <= q0 + lax.broadcasted_iota(I32, (tq, ncmp), 0)
    ce1 = CMP_STRIDE * lax.broadcasted_iota(I32, (1, ncmp), 1) + CMP_BLOCK - 1
    aug_c = jnp.concatenate(
        [zeros_blk[:, 0:ncmp],
         _bias_rows(jnp.bitwise_and(ce1, 255).astype(F32),
                    (lax.shift_right_logical(ce1, 8) - lax.shift_right_logical(q0, 8)).astype(F32),
                    ncmp)], axis=0)
    nsb = 64
    jt = lax.broadcasted_iota(I32, (nsb, ncmp), 0)
    ct = lax.broadcasted_iota(I32, (nsb, ncmp), 1)
    ov_t = jnp.where(CMP_STRIDE * ct < SEL_BLOCK * (jt + 1),
                     jnp.where(CMP_STRIDE * ct + CMP_BLOCK - 1 >= SEL_BLOCK * jt, 1.0, 0.0),
                     0.0).astype(BF16)
    jq = lax.broadcasted_iota(I32, (nsb, tq), 0)
    cur = lax.shift_right_logical(q0 + lax.broadcasted_iota(I32, (nsb, tq), 1), 6)
    elig = jq <= cur
    forced = (jq == 0) | (jq == cur) | (jq == cur - 1)
    n_elig = (i + 1) * (tq // SEL_BLOCK)

    def cmp_group(g, carry):
        r0 = pl.multiple_of(g * HEAD_DIM, HEAD_DIM)
        kc_aug = jnp.concatenate([cmp_ref[0, pl.ds(r0, HEAD_DIM), :], aug_c], axis=0)
        vc_aug = jnp.concatenate([cmp_ref[1, pl.ds(r0, HEAD_DIM), :], zeros_blk[:, 0:ncmp]], axis=0)
        psum = jnp.zeros((tq, ncmp), F32)
        for h in range(HEADS_PER_GROUP):
            hd = g * HEADS_PER_GROUP + h
            s = jnp.where(mask_c, _dot(qaug[hd], kc_aug), NEG)
            m = jnp.max(s, axis=-1, keepdims=True)
            e = jnp.where(mask_c, jnp.exp2(s - m), 0.0)
            p = e / jnp.maximum(jnp.sum(e, axis=-1, keepdims=True), 1e-30)
            psum = psum + p
            acc[hd] = _dot_nt(p.astype(BF16), vc_aug)
        p_hi = psum.astype(BF16)
        p_lo = (psum - p_hi.astype(F32)).astype(BF16)
        imp_t = _dot_nt(ov_t, p_hi) + _dot_nt(ov_t, p_lo)
        sc = jnp.where(elig, jnp.where(forced, FORCE_SCORE, imp_t), NEG)
        scs[...] = sc

        def rank_body(ii, rank):
            row = scs[pl.ds(ii, 1), :]
            beats = jnp.where(row > sc, 1, jnp.where(row == sc, jnp.where(ii < jq, 1, 0), 0))
            return rank + beats

        rank = lax.fori_loop(0, n_elig, rank_body, jnp.zeros((nsb, tq), I32))
        neg = jnp.where(elig, jnp.where(rank < N_SELECT, 0.0, 1.0), 1.0)
        negs[pl.ds(pl.multiple_of(g * nsb, nsb), nsb), :] = neg
        return carry

    lax.fori_loop(0, N_KV_GROUPS, cmp_group, 0)

    for hd in range(N_HEADS):
        res[hd] = gate_ref[:, 3 * hd:3 * hd + 1] * acc[hd]

    neg_q = negs[...].T
    for hd in range(N_HEADS):
        g = hd // HEADS_PER_GROUP
        w = neg_q[:, (g // 2) * LANES:(g // 2 + 1) * LANES]
        if g % 2 == 0:
            w = pltpu.roll(w, HEAD_DIM, 1)
        cur_q = qaug[hd, :, 0:LANES].astype(F32)
        qaug[hd, :, 0:LANES] = jnp.where(lo_half, cur_q, w).astype(BF16)

    def reset_state():
        for hd in range(N_HEADS):
            mst[hd] = jnp.full((tq, LANES), NEG, F32)
            acc[hd] = jnp.zeros((tq, LANES), F32)

    def finalize(branch):
        for hd in range(N_HEADS):
            a = acc[hd]
            f = gate_ref[:, 3 * hd + branch:3 * hd + branch + 1] / a[:, HEAD_DIM:HEAD_DIM + 1]
            res[hd] = res[hd] + a * f

    krel = lax.broadcasted_iota(I32, (1, tk), 1).astype(F32)
    jrow = lax.broadcasted_iota(I32, (HEAD_DIM, tk), 0)
    kcol = lax.broadcasted_iota(I32, (HEAD_DIM, tk), 1)

    def tile_aug(kt, onehot):
        k0 = kt * tk
        dk = (lax.shift_right_logical(k0, 8) - lax.shift_right_logical(q0, 8)).astype(F32)
        bias = _bias_rows(krel, dk, tk)
        if onehot:
            blk = lax.shift_right_logical(k0 + kcol, 6)
            mid = jnp.where(jrow == blk, NEG, 0.0).astype(BF16)
        else:
            mid = zeros_blk
        return jnp.concatenate([mid, bias], axis=0)

    def run_tile(kt, slot_k, onehot, masked):
        aug = tile_aug(kt, onehot)
        k0 = pl.multiple_of(kt * tk, tk)

        def pair_body(gp, carry):
            for gg in range(_GROUPS_PER_BODY):
                g = gp * _GROUPS_PER_BODY + gg
                r0 = pl.multiple_of(g * HEAD_DIM, HEAD_DIM)
                k_t = kvt_ref[pl.ds(slot_k * D_KV + r0, HEAD_DIM), pl.ds(k0, tk)]
                v_t = kvt_ref[pl.ds((slot_k + 1) * D_KV + r0, HEAD_DIM), pl.ds(k0, tk)]
                k_aug = jnp.concatenate([k_t, aug], axis=0)
                v_aug = jnp.concatenate([v_t, ones_blk[...]], axis=0)
                for h in range(HEADS_PER_GROUP):
                    hd = g * HEADS_PER_GROUP + h
                    s = _dot(qaug[hd], k_aug)
                    if masked is not None:
                        s = jnp.where(masked, NEG, s)
                    _online_update(hd, s, v_aug, acc, mst, nrep)
            return carry

        lax.fori_loop(0, N_KV_GROUPS // _GROUPS_PER_BODY, pair_body, 0)

    reset_state()
    lax.fori_loop(0, i, lambda kt, c: (run_tile(kt, 0, True, None), c)[1], 0)
    run_tile(i, 0, True, above)
    finalize(1)

    reset_state()

    def win_tile(kt, mode):
        run_tile(kt, 2, False, {"diag": above, "far": below, "mid": None}[mode])

    nback = WINDOW // tk

    @pl.when(i >= nback)
    def _():
        win_tile(i - nback, "far")

    for back in range(nback - 1, 0, -1):
        @pl.when(i >= back)
        def _():
            win_tile(i - back, "mid")

    win_tile(i, "diag")
    finalize(2)

    for w in range(N_HEADS // 2):
        ow = jnp.where(lo_half, res[2 * w], pltpu.roll(res[2 * w + 1], HEAD_DIM, 1))
        osc[:, w * LANES:(w + 1) * LANES] = (ow * zg_ref[:, w * LANES:(w + 1) * LANES]).astype(BF16)
    xn = x_ref[...] + _dot(osc[...], wout_ref[...])
    if final:
        xn = _rms(xn, fg_ref[...])
    o_ref[...] = xn


def nsa_prompt_attn(q, kvtb, cmpt, gates, zg, x, wout_bf, fg, *, final, tq=256):
    b, t, _ = q.shape
    ncmp = cmpt.shape[3]
    blk = lambda bi, i: (bi, i, 0)
    return pl.pallas_call(
        functools.partial(_nsa_prompt_body, tq=tq, final=final),
        grid=(b, t // tq),
        in_specs=[pl.BlockSpec((None, tq, D_MODEL), blk),
                  pl.BlockSpec((None, 4 * D_KV, t), lambda bi, i: (bi, 0, 0)),
                  pl.BlockSpec((None, 2, D_KV, ncmp), lambda bi, i: (bi, 0, 0, 0)),
                  pl.BlockSpec((None, tq, LANES), blk),
                  pl.BlockSpec((None, tq, D_MODEL), blk),
                  pl.BlockSpec((None, tq, D_MODEL), blk),
                  pl.BlockSpec((D_MODEL, D_MODEL), lambda bi, i: (0, 0)),
                  pl.BlockSpec((1, D_MODEL), lambda bi, i: (0, 0))],
        out_specs=pl.BlockSpec((None, tq, D_MODEL), blk),
        out_shape=jax.ShapeDtypeStruct((b, t, D_MODEL), F32),
        scratch_shapes=[pltpu.VMEM((N_HEADS, tq, 2 * LANES), BF16),
                        pltpu.VMEM((N_HEADS, tq, LANES), F32),
                        pltpu.VMEM((N_HEADS, tq, LANES), F32),
                        pltpu.VMEM((N_HEADS, tq, LANES), F32),
                        pltpu.VMEM((tq, D_MODEL), BF16),
                        pltpu.VMEM((64, tq), F32),
                        pltpu.VMEM((N_KV_GROUPS * 64, tq), F32),
                        pltpu.VMEM((HEAD_DIM, tq), BF16)],
        compiler_params=_params("arbitrary", "arbitrary"),
        name="nsa_prompt_attn",
    )(q, kvtb, cmpt, gates, zg, x, wout_bf, fg)


def _nsa_sample_body(pt_ref, q_ref, kvn_ref, gate_ref, zg_ref, win_ref, w1q_ref, w2h_ref, c_ref,
                     cache_ref, o_ref, pbuf, sem, cmplo, cmphi, ebig, bias_c, bias_s, bias_w,
                     *, layer, npages):
    b = pl.program_id(0)
    nb = pl.num_programs(0)
    past = npages * PAGE_SIZE
    nchunk = past // CMP_STRIDE
    nsel = past + LANES
    nwin = WINDOW + LANES
    rows = N_HEADS * SPAD

    def page_copy(seq, p, slot):
        return pltpu.make_async_copy(cache_ref.at[layer, pt_ref[seq, p]],
                                     pbuf.at[slot, :, :, pl.ds(p * PAGE_SIZE, PAGE_SIZE)],
                                     sem.at[slot])

    def start_seq(seq, slot):
        for p in range(npages):
            page_copy(seq, p, slot).start()

    slot = lax.rem(b, 2)

    @pl.when(b == 0)
    def _():
        start_seq(0, 0)
        r = lax.broadcasted_iota(I32, (rows, 1), 0)
        slope = jnp.exp2(-0.5 * (lax.shift_right_logical(r, 3) + 1).astype(F32)) * LOG2E
        qpos = past + jnp.bitwise_and(r, SPAD - 1)
        ccol = lax.broadcasted_iota(I32, (rows, nchunk), 1)
        c_end = CMP_STRIDE * ccol + CMP_BLOCK - 1
        bias_c[...] = jnp.where(ccol < nchunk - 1, -slope * (qpos - c_end).astype(F32), NEG)
        kpos = lax.broadcasted_iota(I32, (rows, nsel), 1)
        bias_s[...] = jnp.where(kpos <= qpos, -slope * (qpos - kpos).astype(F32), NEG)
        wcol = lax.broadcasted_iota(I32, (rows, nwin), 1)
        wpos = jnp.where(wcol < WINDOW, past - WINDOW + wcol, past + wcol - WINDOW)
        dist_w = qpos - wpos
        ok = (dist_w >= 0) & (dist_w <= WINDOW) & (wcol < WINDOW + SPAD)
        bias_w[...] = jnp.where(ok, -slope * dist_w.astype(F32), NEG)
        ej = lax.broadcasted_iota(I32, (64, nsel), 0)
        ek = lax.shift_right_logical(lax.broadcasted_iota(I32, (64, nsel), 1), 6)
        ebig[...] = jnp.where(ej == ek, NEG, 0.0).astype(BF16)

    @pl.when(b + 1 < nb)
    def _():
        start_seq(b + 1, 1 - slot)

    for p in range(npages):
        page_copy(b, p, slot).wait()

    cmp_out = []
    for w in range(4):
        s, hf = divmod(w, 2)
        tok = pbuf[slot, s, hf * LANES:(hf + 1) * LANES, :].T
        tok3 = tok.reshape(nchunk, CMP_STRIDE, LANES)
        cmplo[w] = tok3[:, 0:8, :].reshape(nchunk * 8, LANES)
        cmphi[w] = tok3[:, 8:16, :].reshape(nchunk * 8, LANES)

        def load_x(l, w=w):
            src = cmplo if l < 8 else cmphi
            return src[w, pl.ds(l % 8, nchunk, stride=8), :]

        hids = _compress_hidden(load_x, w1q_ref.at[s], c_ref[s, 0:1, :], nchunk)
        out = _dot(hids[0], w2h_ref[s, 0]) + _dot(hids[1], w2h_ref[s, 1])
        rix = lax.broadcasted_iota(I32, (nchunk, LANES), 0)
        cmp_out.append(jnp.where(rix < nchunk - 1, out, 0.0))
    kc = jnp.concatenate(cmp_out[0:2], axis=1).astype(BF16)
    vc = jnp.concatenate(cmp_out[2:4], axis=1).astype(BF16)

    qf = q_ref[...].astype(F32)
    lane4 = lax.broadcasted_iota(I32, (SPAD, D_KV), 1)
    pieces = []
    for hd in range(N_HEADS):
        g, h = divmod(hd, HEADS_PER_GROUP)
        w = qf[:, g * D_KV:(g + 1) * D_KV]
        sh = ((g - h) * HEAD_DIM) % D_KV
        if sh:
            w = pltpu.roll(w, sh, 1)
        inside = (lane4 >= g * HEAD_DIM) & (lane4 < (g + 1) * HEAD_DIM)
        pieces.append(jnp.where(inside, w, 0.0))
    qbd = jnp.concatenate(pieces, axis=0).astype(BF16)

    def new_rows(c):
        new = kvn_ref[:, c * D_KV:(c + 1) * D_KV].astype(F32)
        return jnp.concatenate([new, jnp.zeros((LANES - SPAD, D_KV), F32)], axis=0).astype(BF16)

    s_c = _dot_nt(qbd, kc) + bias_c[...]
    m_c = jnp.max(s_c, axis=-1, keepdims=True)
    e_c = jnp.exp2(s_c - m_c)
    p_c = e_c / jnp.maximum(jnp.sum(e_c, axis=-1, keepdims=True), 1e-30)
    o_c = _dot(p_c.astype(BF16), vc)

    psum = jnp.concatenate(
        [sum(p_c[(g * HEADS_PER_GROUP + h) * SPAD:(g * HEADS_PER_GROUP + h + 1) * SPAD]
             for h in range(HEADS_PER_GROUP)) for g in range(N_KV_GROUPS)], axis=0)
    cj = lax.broadcasted_iota(I32, (nchunk, 64), 0)
    jj = lax.broadcasted_iota(I32, (nchunk, 64), 1)
    ov = jnp.where(CMP_STRIDE * cj < SEL_BLOCK * (jj + 1),
                   jnp.where(CMP_STRIDE * cj + CMP_BLOCK - 1 >= SEL_BLOCK * jj, 1.0, 0.0),
                   0.0).astype(BF16)
    p_hi = psum.astype(BF16)
    p_lo = (psum - p_hi.astype(F32)).astype(BF16)
    imp = _dot(p_hi, ov) + _dot(p_lo, ov)
    nrow = N_KV_GROUPS * SPAD
    jb = lax.broadcasted_iota(I32, (nrow, 64), 1)
    cur = lax.shift_right_logical(
        past + jnp.bitwise_and(lax.broadcasted_iota(I32, (nrow, 64), 0), SPAD - 1), 6)
    elig = jb <= cur
    forced = (jb == 0) | (jb == cur) | (jb == cur - 1)
    sc = jnp.where(elig, jnp.where(forced, FORCE_SCORE, imp), NEG)
    rank = jnp.zeros((nrow, 64), I32)
    for ii in range(past // SEL_BLOCK + 1):
        col = sc[:, ii:ii + 1]
        rank = rank + jnp.where(col > sc, 1, jnp.where(col == sc, jnp.where(ii < jb, 1, 0), 0))
    neg = jnp.where(elig, jnp.where(rank < N_SELECT, 0.0, 1.0), 1.0)
    neg_rows = jnp.concatenate(
        [neg[(hd // HEADS_PER_GROUP) * SPAD:(hd // HEADS_PER_GROUP + 1) * SPAD] for hd in range(N_HEADS)],
        axis=0).astype(BF16)

    k_new, v_new = new_rows(0), new_rows(1)
    s_s = (jnp.concatenate([_dot(qbd, pbuf[slot, 2].astype(BF16)), _dot_nt(qbd, k_new)], axis=1)
           + _dot(neg_rows, ebig[...]) + bias_s[...])
    m_s = jnp.max(s_s, axis=-1, keepdims=True)
    e_s = jnp.exp2(s_s - m_s)
    e_sb = e_s.astype(BF16)
    o_s = (_dot_nt(e_sb[:, 0:past], pbuf[slot, 3].astype(BF16)) + _dot(e_sb[:, past:], v_new)
           ) / jnp.sum(e_s, axis=-1, keepdims=True)

    kw_new, vw_new = new_rows(2), new_rows(3)
    s_w = jnp.concatenate([_dot(qbd, win_ref[0].astype(BF16)), _dot_nt(qbd, kw_new)], axis=1) + bias_w[...]
    m_w = jnp.max(s_w, axis=-1, keepdims=True)
    e_w = jnp.exp2(s_w - m_w)
    e_wb = e_w.astype(BF16)
    o_w = (_dot_nt(e_wb[:, 0:WINDOW], win_ref[1].astype(BF16)) + _dot(e_wb[:, WINDOW:], vw_new)
           ) / jnp.sum(e_w, axis=-1, keepdims=True)

    gates = gate_ref[...]
    gcol = lambda br: jnp.concatenate(
        [gates[:, 3 * hd + br:3 * hd + br + 1] for hd in range(N_HEADS)], axis=0)
    o_all = gcol(0) * o_c + gcol(1) * o_s + gcol(2) * o_w

    outs = []
    for g in range(N_KV_GROUPS):
        inside = (lane4 >= g * HEAD_DIM) & (lane4 < (g + 1) * HEAD_DIM)
        tot = jnp.zeros((SPAD, D_KV), F32)
        for h in range(HEADS_PER_GROUP):
            hd = g * HEADS_PER_GROUP + h
            piece = jnp.where(inside, o_all[hd * SPAD:(hd + 1) * SPAD], 0.0)
            sh = ((h - g) * HEAD_DIM) % D_KV
            if sh:
                piece = pltpu.roll(piece, sh, 1)
            tot = tot + piece
        outs.append(tot)
    o_ref[...] = jnp.concatenate(outs, axis=1) * zg_ref[...]


def nsa_sample_attn(page_table, q, kvn, gates, zg, win_t, w1q, w2h, const, cache_t, *, layer):
    nb, npages = page_table.shape
    past = npages * PAGE_SIZE
    nchunk = past // CMP_STRIDE
    rows = N_HEADS * SPAD
    seq = lambda i, pt: (i, 0)
    grid_spec = pltpu.PrefetchScalarGridSpec(
        num_scalar_prefetch=1,
        grid=(nb,),
        in_specs=[pl.BlockSpec((SPAD, D_MODEL), seq),
                  pl.BlockSpec((SPAD, 4 * D_KV), seq),
                  pl.BlockSpec((SPAD, LANES), seq),
                  pl.BlockSpec((SPAD, D_MODEL), seq),
                  pl.BlockSpec((None, None, 2, D_KV, WINDOW), lambda i, pt: (layer, i, 0, 0, 0)),
                  pl.BlockSpec((2, 4, 256, 256), lambda i, pt: (0, 0, 0, 0)),
                  pl.BlockSpec((2, 2, CMP_HIDDEN, LANES), lambda i, pt: (0, 0, 0, 0)),
                  pl.BlockSpec((2, 8, CMP_HIDDEN), lambda i, pt: (0, 0, 0)),
                  pl.BlockSpec(memory_space=pl.ANY)],
        out_specs=pl.BlockSpec((SPAD, D_MODEL), seq),
        scratch_shapes=[pltpu.VMEM((2, 4, D_KV, past), F32),
                        pltpu.SemaphoreType.DMA((2,)),
                        pltpu.VMEM((4, nchunk * 8, LANES), F32),
                        pltpu.VMEM((4, nchunk * 8, LANES), F32),
                        pltpu.VMEM((64, past + LANES), BF16),
                        pltpu.VMEM((rows, nchunk), F32),
                        pltpu.VMEM((rows, past + LANES), F32),
                        pltpu.VMEM((rows, WINDOW + LANES), F32)])
    return pl.pallas_call(
        functools.partial(_nsa_sample_body, layer=layer, npages=npages),
        grid_spec=grid_spec,
        out_shape=jax.ShapeDtypeStruct((nb * SPAD, D_MODEL), F32),
        compiler_params=_params("arbitrary"),
        name="nsa_sample_attn",
    )(page_table, q, kvn, gates, zg, win_t, w1q, w2h, const, cache_t)


def _outproj_body(o_ref, x_ref, w_ref, fg_ref, y_ref, *, final):
    xn = x_ref[...] + _dot(o_ref[...].astype(BF16), w_ref[...])
    if final:
        xn = _rms(xn, fg_ref[...])
    y_ref[...] = xn


def outproj(o, x, w_bf, fg, *, final, tm=256):
    n = o.shape[0]
    row = lambda i: (i, 0)
    return pl.pallas_call(
        functools.partial(_outproj_body, final=final),
        grid=(n // tm,),
        in_specs=[pl.BlockSpec((tm, D_MODEL), row), pl.BlockSpec((tm, D_MODEL), row),
                  pl.BlockSpec((D_MODEL, D_MODEL), lambda i: (0, 0)),
                  pl.BlockSpec((1, D_MODEL), lambda i: (0, 0))],
        out_specs=pl.BlockSpec((tm, D_MODEL), row),
        out_shape=jax.ShapeDtypeStruct((n, D_MODEL), F32),
        compiler_params=_params("arbitrary"),
        name="outproj",
    )(o, x, w_bf, fg)


def _prep_cmp_weights(pe, w1, w2):
    top = w1[:, :CMP_STRIDE].reshape(2, 4, 4 * HEAD_DIM, CMP_HIDDEN)
    bot = w1[:, CMP_STRIDE:].reshape(2, 4, 4 * HEAD_DIM, CMP_HIDDEN)
    w1q = jnp.concatenate([top, bot], axis=-1).astype(BF16)
    w2h = jnp.zeros((2, 2, CMP_HIDDEN, LANES), F32)
    for hf in range(2):
        w2h = w2h.at[:, hf, :, hf * HEAD_DIM:(hf + 1) * HEAD_DIM].set(w2)
    w2t = jnp.swapaxes(w2h, 2, 3)
    pe_pad = jnp.zeros((2, 8, CMP_BLOCK * HEAD_DIM), F32).at[:, 0].set(pe.reshape(2, -1))
    w1_flat = w1.reshape(2, CMP_BLOCK * HEAD_DIM, CMP_HIDDEN)
    return w1q, w2h.astype(BF16), w2t.astype(BF16), pe_pad.astype(BF16), w1_flat.astype(BF16)


def _prep_nsa_in_weights(w_in):
    gl = jnp.pad(w_in[:, 3584:], ((0, 0), (0, LANES - N_BRANCH * N_HEADS)))
    w_a = jnp.concatenate([w_in[:, 0:2048], w_in[:, 2048:2560], gl], axis=1).astype(BF16)
    w_t = w_in[:, 2048:3584].T.astype(BF16)
    w_n = w_in[:, 2560:3584].astype(BF16)
    return w_a, w_t, w_n


def kernel(x_prompt, x_sample, cache_kv, state_kv_win, state_conv, page_table, norm_g, final_norm_g,
           conv_w_in, conv_dw_w, conv_dw_b, conv_ln_g, conv_ln_b, conv_w_out,
           nsa_w_in, nsa_w_out, nsa_cmp_pe, nsa_cmp_w1, nsa_cmp_w2):
    bsz, seq, _ = x_prompt.shape
    nb = x_sample.shape[0]
    n_nsa = nsa_w_in.shape[0]
    n_pool = cache_kv.shape[1]
    win_keep = state_kv_win.shape[2]
    ns = SPAD * nb
    xp = x_prompt
    xs = jnp.pad(jnp.swapaxes(x_sample, 0, 1), ((0, SPAD - DEC_SEQ), (0, 0), (0, 0)))
    cache_t = jnp.transpose(cache_kv, (0, 1, 3, 4, 5, 2)).reshape(n_nsa, n_pool, 4, D_KV, PAGE_SIZE)
    win_state_t = jnp.transpose(state_kv_win, (0, 1, 3, 4, 5, 2)).reshape(n_nsa, nb, 2, D_KV, win_keep)
    conv_state_t = jnp.swapaxes(state_conv, 1, 2)
    fg = final_norm_g.reshape(1, D_MODEL)

    kv_p, kv_s, win_p, win_s, conv_p, conv_s = [], [], [], [], [], []
    for layer in range(DEPTH):
        j = layer // 2
        g = norm_g[layer].reshape(1, D_MODEL)
        last = layer == DEPTH - 1
        if layer % 2 == 0:
            w_in = conv_w_in[j].astype(BF16)
            w_out = conv_w_out[j].astype(BF16)
            db = conv_dw_b[j].reshape(1, D_CONV)
            lg = conv_ln_g[j].reshape(1, D_CONV)
            lb = conv_ln_b[j].reshape(1, D_CONV)
            v, zg = conv_inproj(xp.reshape(bsz * seq, D_MODEL), g, w_in)
            v = v.reshape(bsz, seq, D_CONV)
            xp = conv_prompt(v, zg.reshape(bsz, seq, D_CONV), xp, conv_dw_w[j], db, lg, lb, w_out)
            conv_p.append(v[:, seq - CONV_STATE:])
            vs, zgs = conv_inproj(xs.reshape(ns, D_MODEL), g, w_in)
            vs = vs.reshape(SPAD, nb, D_CONV)
            xs = conv_sample(conv_state_t, vs, zgs.reshape(SPAD, nb, D_CONV), xs, conv_dw_w[j],
                             db, lg, lb, w_out, layer=j)
            new_state_t = jnp.concatenate([conv_state_t[j][DEC_SEQ:], vs[:DEC_SEQ]], axis=0)
            conv_s.append(jnp.swapaxes(new_state_t, 0, 1))
        else:
            w_a, w_t, w_n = _prep_nsa_in_weights(nsa_w_in[j])
            w_out = nsa_w_out[j].astype(BF16)
            w1q, w2h, w2t, pe_pad, w1_flat = _prep_cmp_weights(nsa_cmp_pe[j], nsa_cmp_w1[j], nsa_cmp_w2[j])
            const = cmp_const(pe_pad, w1_flat)
            q, zg, gates, kvpt, kvwt, kcn, kvtb = nsa_inproj(xp, g, w_a, w_t)
            cmpt = compress_prompt(kcn, w1q, w2t, const)
            xp = nsa_prompt_attn(q, kvtb, cmpt, gates, zg, xp, w_out, fg, final=last)
            kv_p.append(jnp.transpose(kvpt.reshape(bsz, 4, N_KV_GROUPS, HEAD_DIM, seq), (0, 4, 1, 2, 3)))
            wtail = kvwt.reshape(bsz, 2, N_KV_GROUPS, HEAD_DIM, seq)[..., seq - min(WINDOW, seq):]
            win_p.append(jnp.transpose(wtail, (0, 4, 1, 2, 3)))
            xq = jnp.swapaxes(xs, 0, 1).reshape(ns, D_MODEL)
            qs, zgs, gs, kvpts, kvwts, kvns = nsa_inproj(xq.reshape(1, ns, D_MODEL), g, w_a, w_t, w_n)
            og = nsa_sample_attn(page_table, qs.reshape(ns, D_MODEL), kvns.reshape(ns, 4 * D_KV),
                                 gs.reshape(ns, LANES), zgs.reshape(ns, D_MODEL),
                                 win_state_t, w1q, w2h, const, cache_t, layer=j)
            xq = outproj(og, xq, w_out, fg, final=last)
            xs = jnp.swapaxes(xq.reshape(nb, SPAD, D_MODEL), 0, 1)
            kvps5 = kvpts.reshape(4, N_KV_GROUPS, HEAD_DIM, nb, SPAD)[..., :DEC_SEQ]
            kv_s.append(jnp.transpose(kvps5, (3, 4, 0, 1, 2)))
            kvws5 = kvwts.reshape(2, N_KV_GROUPS, HEAD_DIM, nb, SPAD)[..., :DEC_SEQ]
            new_w = jnp.transpose(kvws5, (3, 4, 0, 1, 2))
            win_s.append(jnp.concatenate([state_kv_win[j], new_w], axis=1)[:, -win_keep:])
    y_sample = jnp.swapaxes(xs[:DEC_SEQ], 0, 1)
    return (xp, y_sample, jnp.stack(kv_p), jnp.stack(kv_s), jnp.stack(win_p), jnp.stack(win_s),
            jnp.stack(conv_p), jnp.stack(conv_s))
```

```python
import functools
import math

import jax
import jax.numpy as jnp
import ml_dtypes
import numpy as np
from jax import lax
from jax.experimental import pallas as pl
from jax.experimental.pallas import tpu as pltpu

F32 = jnp.float32
BF16 = jnp.bfloat16
I32 = jnp.int32

D_MODEL = 1024
DEPTH = 4
DEC_SEQ = 4
PAGE_SIZE = 128
D_CONV = 2048
CONV_WIDTH = 31
CONV_STATE = CONV_WIDTH - 1
N_HEADS = 16
HEAD_DIM = 64
N_KV_GROUPS = 4
HEADS_PER_GROUP = 4
D_KV = N_KV_GROUPS * HEAD_DIM
CMP_BLOCK = 32
CMP_STRIDE = 16
CMP_HIDDEN = 128
SEL_BLOCK = 64
N_SELECT = 16
WINDOW = 512
N_BRANCH = 3
NORM_EPS = 1e-6
FORCE_SCORE = 1e4
NEG = -1e30
LOG2E = math.log2(math.e)
SPAD = 8
LANES = 128
VMEM_LIMIT = 56 * 1024 * 1024


def _params(*sem):
    return pltpu.CompilerParams(dimension_semantics=sem, vmem_limit_bytes=VMEM_LIMIT)


def _rms(x, g):
    return x * lax.rsqrt(jnp.mean(x * x, axis=-1, keepdims=True) + NORM_EPS) * g


def _silu(x):
    return x * jax.nn.sigmoid(x)


def _dot(a, b):
    return jnp.dot(a, b, preferred_element_type=F32)


def _dot_nt(a, b):
    return lax.dot_general(a, b, (((1,), (1,)), ((), ())), preferred_element_type=F32)


def _rep(x, n):
    return jnp.concatenate([x] * n, axis=1) if n > 1 else x


def _slope2(hd):
    return 2.0 ** (-(hd + 1) / 2.0) * LOG2E


def _hi_lo(x):
    hi = float(np.float32(x).astype(ml_dtypes.bfloat16).astype(np.float32))
    lo = float(np.float32(x - hi).astype(ml_dtypes.bfloat16).astype(np.float32))
    return hi, lo


def _conv_inproj_body(x_ref, g_ref, w_ref, v_ref, zg_ref):
    h = _rms(x_ref[...], g_ref[...]).astype(BF16)
    ch = 512
    for c in range(D_CONV // ch):
        lo = c * ch
        a = _dot(h, w_ref[:, lo:lo + ch])
        gl = _dot(h, w_ref[:, D_CONV + lo:D_CONV + lo + ch])
        z = _dot(h, w_ref[:, 2 * D_CONV + lo:2 * D_CONV + lo + ch])
        v_ref[:, lo:lo + ch] = a * jax.nn.sigmoid(gl)
        zg_ref[:, lo:lo + ch] = _silu(z)


def conv_inproj(x, g, w_bf, tm=256):
    n = x.shape[0]
    return pl.pallas_call(
        _conv_inproj_body,
        grid=(n // tm,),
        in_specs=[pl.BlockSpec((tm, D_MODEL), lambda i: (i, 0)),
                  pl.BlockSpec((1, D_MODEL), lambda i: (0, 0)),
                  pl.BlockSpec((D_MODEL, 3 * D_CONV), lambda i: (0, 0))],
        out_specs=[pl.BlockSpec((tm, D_CONV), lambda i: (i, 0)),
                   pl.BlockSpec((tm, D_CONV), lambda i: (i, 0))],
        out_shape=[jax.ShapeDtypeStruct((n, D_CONV), F32),
                   jax.ShapeDtypeStruct((n, D_CONV), F32)],
        compiler_params=_params("arbitrary"),
        name="conv_inproj",
    )(x, g, w_bf)


_HDR = 32
_RW = 32
_PH = 4
_NSLAB = D_CONV // LANES
_GROUPS_PER_BODY = 2


def _ln_gate(c, lg, lb, zg):
    mu = jnp.mean(c, axis=-1, keepdims=True)
    xc = c - mu
    var = jnp.mean(xc * xc, axis=-1, keepdims=True)
    y = xc * lax.rsqrt(var + NORM_EPS) * lg + lb
    return _silu(y) * zg


def _conv_prompt_body(v_ref, zg_ref, x_ref, dw_ref, db_ref, lg_ref, lb_ref, wout_ref, o_ref,
                      vbuf, cbuf, ybuf, *, tt):
    i = pl.program_id(1)

    @pl.when(i == 0)
    def _():
        vbuf[:, 0:_HDR, :] = jnp.zeros((_NSLAB, _HDR, LANES), F32)

    @pl.when(i > 0)
    def _():
        vbuf[:, 0:_HDR, :] = vbuf[:, tt:tt + _HDR, :]

    for w in range(_NSLAB):
        vbuf[w, _HDR:_HDR + tt, :] = v_ref[:, w * LANES:(w + 1) * LANES]
    off = _HDR - CONV_STATE

    def row_body(r, carry):
        r0 = pl.multiple_of(r * _RW, _RW)

        def slab_body(w, carry2):
            c0 = pl.multiple_of(w * LANES, LANES)
            accs = [[None, None] for _ in range(_PH)]
            for j in range(CONV_WIDTH):
                wj = dw_ref[pl.ds(j, 1), pl.ds(c0, LANES)]
                for u in range(_PH):
                    term = vbuf[w, pl.ds(r0 + u + j + off, 8, stride=_PH), :] * wj
                    prev = accs[u][j % 2]
                    accs[u][j % 2] = term if prev is None else prev + term
            bias = db_ref[:, pl.ds(c0, LANES)]
            for u in range(_PH):
                cbuf[w, pl.ds(r0 + u, 8, stride=_PH), :] = (accs[u][0] + bias) + accs[u][1]
            return carry2

        lax.fori_loop(0, _NSLAB, slab_body, 0)
        c = jnp.concatenate([cbuf[w, pl.ds(r0, _RW), :] for w in range(_NSLAB)], axis=1)
        y = _ln_gate(c, lg_ref[...], lb_ref[...], zg_ref[pl.ds(r0, _RW), :])
        ybuf[pl.ds(r0, _RW), :] = y.astype(BF16)
        return carry

    lax.fori_loop(0, tt // _RW, row_body, 0)
    o_ref[...] = x_ref[...] + _dot(ybuf[...], wout_ref[...])


def conv_prompt(v, zg, x, dw, db, lg, lb, wout_bf, tt=512):
    b, t, _ = v.shape
    return pl.pallas_call(
        functools.partial(_conv_prompt_body, tt=tt),
        grid=(b, t // tt),
        in_specs=[pl.BlockSpec((None, tt, D_CONV), lambda bi, i: (bi, i, 0)),
                  pl.BlockSpec((None, tt, D_CONV), lambda bi, i: (bi, i, 0)),
                  pl.BlockSpec((None, tt, D_MODEL), lambda bi, i: (bi, i, 0)),
                  pl.BlockSpec((CONV_WIDTH, D_CONV), lambda bi, i: (0, 0)),
                  pl.BlockSpec((1, D_CONV), lambda bi, i: (0, 0)),
                  pl.BlockSpec((1, D_CONV), lambda bi, i: (0, 0)),
                  pl.BlockSpec((1, D_CONV), lambda bi, i: (0, 0)),
                  pl.BlockSpec((D_CONV, D_MODEL), lambda bi, i: (0, 0))],
        out_specs=pl.BlockSpec((None, tt, D_MODEL), lambda bi, i: (bi, i, 0)),
        out_shape=jax.ShapeDtypeStruct((b, t, D_MODEL), F32),
        scratch_shapes=[pltpu.VMEM((_NSLAB, tt + _HDR, LANES), F32),
                        pltpu.VMEM((_NSLAB, tt, LANES), F32),
                        pltpu.VMEM((tt, D_CONV), BF16)],
        compiler_params=_params("arbitrary", "arbitrary"),
        name="conv_prompt",
    )(v, zg, x, dw, db, lg, lb, wout_bf)


def _conv_sample_body(st_ref, v_ref, zg_ref, x_ref, dw_ref, db_ref, lg_ref, lb_ref, wout_ref,
                      o_ref, ybuf, *, bs):
    ch = 512
    for t in range(DEC_SEQ):
        parts = []
        for c in range(D_CONV // ch):
            cs = slice(c * ch, (c + 1) * ch)
            acc = jnp.zeros((bs, ch), F32)
            for j in range(CONV_WIDTH):
                r = t + j
                src = st_ref[r, :, cs] if r < CONV_STATE else v_ref[r - CONV_STATE, :, cs]
                acc = acc + src * dw_ref[j:j + 1, cs]
            parts.append(acc)
        c_t = jnp.concatenate(parts, axis=1) + db_ref[...]
        ybuf[t * bs:(t + 1) * bs, :] = _ln_gate(c_t, lg_ref[...], lb_ref[...], zg_ref[t])
    y = _dot(ybuf[...].astype(BF16), wout_ref[...])
    for t in range(DEC_SEQ):
        o_ref[t] = x_ref[t] + y[t * bs:(t + 1) * bs]
    for t in range(DEC_SEQ, SPAD):
        o_ref[t] = jnp.zeros((bs, D_MODEL), F32)


def conv_sample(state_t, v, zg, x, dw, db, lg, lb, wout_bf, *, layer, bs=16):
    nb = state_t.shape[2]
    blk = lambda i: (0, i, 0)
    return pl.pallas_call(
        functools.partial(_conv_sample_body, bs=bs),
        grid=(nb // bs,),
        in_specs=[pl.BlockSpec((None, CONV_STATE, bs, D_CONV), lambda i: (layer, 0, i, 0)),
                  pl.BlockSpec((DEC_SEQ, bs, D_CONV), blk),
                  pl.BlockSpec((DEC_SEQ, bs, D_CONV), blk),
                  pl.BlockSpec((SPAD, bs, D_MODEL), blk),
                  pl.BlockSpec((CONV_WIDTH, D_CONV), lambda i: (0, 0)),
                  pl.BlockSpec((1, D_CONV), lambda i: (0, 0)),
                  pl.BlockSpec((1, D_CONV), lambda i: (0, 0)),
                  pl.BlockSpec((1, D_CONV), lambda i: (0, 0)),
                  pl.BlockSpec((D_CONV, D_MODEL), lambda i: (0, 0))],
        out_specs=pl.BlockSpec((SPAD, bs, D_MODEL), blk),
        out_shape=jax.ShapeDtypeStruct((SPAD, nb, D_MODEL), F32),
        scratch_shapes=[pltpu.VMEM((DEC_SEQ * bs, D_CONV), F32)],
        compiler_params=_params("arbitrary"),
        name="conv_sample",
    )(state_t, v, zg, x, dw, db, lg, lb, wout_bf)


def _nsa_inproj_body(*refs, sample):
    if sample:
        x_ref, g_ref, wa_ref, wt_ref, wn_ref, q_ref, zg_ref, gate_ref, kvp_ref, kvw_ref, kvn_ref = refs
    else:
        x_ref, g_ref, wa_ref, wt_ref, q_ref, zg_ref, gate_ref, kvp_ref, kvw_ref, kcn_ref, kvtb_ref = refs
    h = _rms(x_ref[...], g_ref[...]).astype(BF16)
    ch = 512
    for c in range(2):
        q = _dot(h, wa_ref[:, c * ch:(c + 1) * ch])
        q_ref[:, c * ch:(c + 1) * ch] = (q * (HEAD_DIM ** -0.5 * LOG2E)).astype(BF16)
    for c in range(2):
        z = _dot(h, wa_ref[:, D_MODEL + c * ch:D_MODEL + (c + 1) * ch])
        zg_ref[:, c * ch:(c + 1) * ch] = _silu(z)
    gate_ref[...] = jax.nn.sigmoid(_dot(h, wa_ref[:, 2560:2688]))
    for c in range(3):
        kvt = _dot_nt(wt_ref[c * ch:(c + 1) * ch, :], h)
        if c < 2:
            kvp_ref[c * ch:(c + 1) * ch, :] = kvt
        else:
            kvw_ref[...] = kvt
        if not sample and c >= 1:
            kvtb_ref[(c - 1) * ch:c * ch, :] = kvt.astype(BF16)
    if sample:
        for c in range(2):
            kvn_ref[:, c * ch:(c + 1) * ch] = _dot(h, wn_ref[:, c * ch:(c + 1) * ch]).astype(BF16)
    else:
        kcn_ref[...] = _dot(h, wa_ref[:, 2048:2560])


def nsa_inproj(x, g, w_a, w_t, w_n=None, tm=256):
    b, t, _ = x.shape
    sample = w_n is not None
    tok = lambda bi, i: (bi, i, 0)
    const = lambda bi, i: (0, 0)
    in_specs = [pl.BlockSpec((None, tm, D_MODEL), tok),
                pl.BlockSpec((1, D_MODEL), const),
                pl.BlockSpec((D_MODEL, 2688), const),
                pl.BlockSpec((6 * D_KV, D_MODEL), const)]
    out_specs = [pl.BlockSpec((None, tm, D_MODEL), tok),
                 pl.BlockSpec((None, tm, D_MODEL), tok),
                 pl.BlockSpec((None, tm, LANES), tok),
                 pl.BlockSpec((None, 4 * D_KV, tm), lambda bi, i: (bi, 0, i)),
                 pl.BlockSpec((None, 2 * D_KV, tm), lambda bi, i: (bi, 0, i))]
    out_shape = [jax.ShapeDtypeStruct((b, t, D_MODEL), BF16),
                 jax.ShapeDtypeStruct((b, t, D_MODEL), F32),
                 jax.ShapeDtypeStruct((b, t, LANES), F32),
                 jax.ShapeDtypeStruct((b, 4 * D_KV, t), F32),
                 jax.ShapeDtypeStruct((b, 2 * D_KV, t), F32)]
    args = [x, g, w_a, w_t]
    if sample:
        in_specs.append(pl.BlockSpec((D_MODEL, 4 * D_KV), const))
        out_specs.append(pl.BlockSpec((None, tm, 4 * D_KV), tok))
        out_shape.append(jax.ShapeDtypeStruct((b, t, 4 * D_KV), BF16))
        args.append(w_n)
    else:
        out_specs.append(pl.BlockSpec((None, tm, 2 * D_KV), tok))
        out_shape.append(jax.ShapeDtypeStruct((b, t, 2 * D_KV), F32))
        out_specs.append(pl.BlockSpec((None, 4 * D_KV, tm), lambda bi, i: (bi, 0, i)))
        out_shape.append(jax.ShapeDtypeStruct((b, 4 * D_KV, t), BF16))
    return pl.pallas_call(
        functools.partial(_nsa_inproj_body, sample=sample),
        grid=(b, t // tm),
        in_specs=in_specs, out_specs=out_specs, out_shape=out_shape,
        compiler_params=_params("arbitrary", "arbitrary"),
        name="nsa_inproj_sample" if sample else "nsa_inproj",
    )(*args)


def _cmp_const_body(pe_ref, w1_ref, o_ref):
    o_ref[...] = _dot(pe_ref[...], w1_ref[...])


def cmp_const(pe_pad_bf, w1_flat_bf):
    return pl.pallas_call(
        _cmp_const_body,
        grid=(2,),
        in_specs=[pl.BlockSpec((None, 8, 2048), lambda s: (s, 0, 0)),
                  pl.BlockSpec((None, 2048, CMP_HIDDEN), lambda s: (s, 0, 0))],
        out_specs=pl.BlockSpec((None, 8, CMP_HIDDEN), lambda s: (s, 0, 0)),
        out_shape=jax.ShapeDtypeStruct((2, 8, CMP_HIDDEN), F32),
        compiler_params=_params("arbitrary"),
        name="cmp_const",
    )(pe_pad_bf, w1_flat_bf)


def _compress_hidden(load_x, w1q_ref, const_row, n):
    accs = [jnp.zeros((n, 2 * CMP_HIDDEN), F32) for _ in range(2)]
    for qd in range(4):
        xs = [load_x(4 * qd + j) for j in range(4)]
        w = w1q_ref[qd]
        for hf in range(2):
            lhs = jnp.concatenate([x[:, hf * HEAD_DIM:(hf + 1) * HEAD_DIM] for x in xs], axis=1)
            accs[hf] = accs[hf] + _dot(lhs.astype(BF16), w)
    hids = []
    for hf in range(2):
        first = accs[hf][:, :CMP_HIDDEN]
        second = pltpu.roll(accs[hf][:, CMP_HIDDEN:], n - 1, 0)
        hids.append(_silu(first + second + const_row).astype(BF16))
    return hids


def _compress_prompt_body(x_ref, w1q_ref, w2t_ref, c_ref, o_ref, *, n):
    load_x = lambda l: x_ref[pl.ds(l, n, stride=CMP_STRIDE), :]
    hids = _compress_hidden(load_x, w1q_ref, c_ref[0:1, :], n)
    out_t = _dot_nt(w2t_ref[0], hids[0]) + _dot_nt(w2t_ref[1], hids[1])
    cols = lax.broadcasted_iota(I32, (LANES, n), 1)
    o_ref[...] = jnp.where(cols < n - 1, out_t, 0.0).astype(BF16)


def compress_prompt(kcn, w1q, w2t, const):
    b, t, _ = kcn.shape
    n = t // CMP_STRIDE
    return pl.pallas_call(
        functools.partial(_compress_prompt_body, n=n),
        grid=(b, 2, 2),
        in_specs=[pl.BlockSpec((None, t, LANES), lambda bi, s, w: (bi, 0, 2 * s + w)),
                  pl.BlockSpec((None, 4, 256, 256), lambda bi, s, w: (s, 0, 0, 0)),
                  pl.BlockSpec((None, 2, LANES, CMP_HIDDEN), lambda bi, s, w: (s, 0, 0, 0)),
                  pl.BlockSpec((None, 8, CMP_HIDDEN), lambda bi, s, w: (s, 0, 0))],
        out_specs=pl.BlockSpec((None, None, LANES, n), lambda bi, s, w: (bi, s, w, 0)),
        out_shape=jax.ShapeDtypeStruct((b, 2, D_KV, n), BF16),
        compiler_params=_params("arbitrary", "arbitrary", "arbitrary"),
        name="compress_prompt",
    )(kcn, w1q, w2t, const)


def _bias_rows(pos_lo, pos_hi, nkeys):
    ri = lax.broadcasted_iota(I32, (LANES, nkeys), 0)
    return jnp.where(ri < 2, pos_lo, jnp.where(ri < 4, pos_hi, 0.0)).astype(BF16)


def _online_update(hd, s, v_aug, acc, mst, nrep):
    m_old = mst[hd]
    m_new = jnp.maximum(m_old, jnp.max(s, axis=-1, keepdims=True))
    p = jnp.exp2(s - _rep(m_new, nrep))
    alpha = jnp.exp2(m_old - m_new)
    acc[hd] = alpha * acc[hd] + _dot_nt(p.astype(BF16), v_aug)
    mst[hd] = m_new


def _nsa_prompt_body(q_ref, kvt_ref, cmp_ref, gate_ref, zg_ref, x_ref, wout_ref, fg_ref,
                     o_ref, qaug, acc, res, mst, osc, scs, negs, ones_blk, *, tq, final):
    i = pl.program_id(1)
    q0 = i * tq
    tk = tq
    nrep = tk // LANES
    lane = lax.broadcasted_iota(I32, (tq, LANES), 1)
    lo_half = lane < HEAD_DIM
    kid = lax.broadcasted_iota(I32, (tq, tk), 1)
    qid = lax.broadcasted_iota(I32, (tq, tk), 0)
    above = kid > qid
    below = kid < qid

    orow = lax.broadcasted_iota(I32, (HEAD_DIM, tk), 0)
    ones_blk[...] = jnp.where(orow == 0, 1.0, 0.0).astype(BF16)
    zeros_blk = jnp.zeros((HEAD_DIM, tk), BF16)

    for hd in range(N_HEADS):
        w = q_ref[:, (hd // 2) * LANES:(hd // 2 + 1) * LANES].astype(F32)
        if hd % 2 == 1:
            w = pltpu.roll(w, HEAD_DIM, 1)
        qaug[hd, :, 0:LANES] = jnp.where(lo_half, w, 0.0).astype(BF16)
        hi, lo = _hi_lo(_slope2(hd))
        cols = jnp.where(lane == 0, hi, jnp.where(lane == 1, lo, jnp.where(
            lane == 2, 256.0 * hi, jnp.where(lane == 3, 256.0 * lo, 0.0))))
        qaug[hd, :, LANES:2 * LANES] = cols.astype(BF16)

    ncmp = cmp_ref.shape[2]
    cid = lax.broadcasted_iota(I32, (tq, ncmp), 1)
    c_end = CMP_STRIDE * cid + CMP_BLOCK - 1
    mask_c = c_end <= q0 + lax.broadcasted_iota(I32, (tq, ncmp), 0)
    ce1 = CMP_STRIDE * lax.broadcasted_iota(I32, (1, ncmp), 1) + CMP_BLOCK - 1
    aug_c = jnp.concatenate(
        [zeros_blk[:, 0:ncmp],
         _bias_rows(jnp.bitwise_and(ce1, 255).astype(F32),
                    (lax.shift_right_logical(ce1, 8) - lax.shift_right_logical(q0, 8)).astype(F32),
                    ncmp)], axis=0)
    nsb = 64
    jt = lax.broadcasted_iota(I32, (nsb, ncmp), 0)
    ct = lax.broadcasted_iota(I32, (nsb, ncmp), 1)
    ov_t = jnp.where(CMP_STRIDE * ct < SEL_BLOCK * (jt + 1),
                     jnp.where(CMP_STRIDE * ct + CMP_BLOCK - 1 >= SEL_BLOCK * jt, 1.0, 0.0),
                     0.0).astype(BF16)
    jq = lax.broadcasted_iota(I32, (nsb, tq), 0)
    cur = lax.shift_right_logical(q0 + lax.broadcasted_iota(I32, (nsb, tq), 1), 6)
    elig = jq <= cur
    forced = (jq == 0) | (jq == cur) | (jq == cur - 1)
    n_elig = (i + 1) * (tq // SEL_BLOCK)

    def cmp_group(g, carry):
        r0 = pl.multiple_of(g * HEAD_DIM, HEAD_DIM)
        kc_aug = jnp.concatenate([cmp_ref[0, pl.ds(r0, HEAD_DIM), :], aug_c], axis=0)
        vc_aug = jnp.concatenate([cmp_ref[1, pl.ds(r0, HEAD_DIM), :], zeros_blk[:, 0:ncmp]], axis=0)
        psum = jnp.zeros((tq, ncmp), F32)
        for h in range(HEADS_PER_GROUP):
            hd = g * HEADS_PER_GROUP + h
            s = jnp.where(mask_c, _dot(qaug[hd], kc_aug), NEG)
            m = jnp.max(s, axis=-1, keepdims=True)
            e = jnp.where(mask_c, jnp.exp2(s - m), 0.0)
            p = e / jnp.maximum(jnp.sum(e, axis=-1, keepdims=True), 1e-30)
            psum = psum + p
            acc[hd] = _dot_nt(p.astype(BF16), vc_aug)
        p_hi = psum.astype(BF16)
        p_lo = (psum - p_hi.astype(F32)).astype(BF16)
        imp_t = _dot_nt(ov_t, p_hi) + _dot_nt(ov_t, p_lo)
        sc = jnp.where(elig, jnp.where(forced, FORCE_SCORE, imp_t), NEG)
        scs[...] = sc

        def rank_body(ii, rank):
            row = scs[pl.ds(ii, 1), :]
            beats = jnp.where(row > sc, 1, jnp.where(row == sc, jnp.where(ii < jq, 1, 0), 0))
            return rank + beats

        rank = lax.fori_loop(0, n_elig, rank_body, jnp.zeros((nsb, tq), I32))
        neg = jnp.where(elig, jnp.where(rank < N_SELECT, 0.0, 1.0), 1.0)
        negs[pl.ds(pl.multiple_of(g * nsb, nsb), nsb), :] = neg
        return carry

    lax.fori_loop(0, N_KV_GROUPS, cmp_group, 0)

    for hd in range(N_HEADS):
        res[hd] = gate_ref[:, 3 * hd:3 * hd + 1] * acc[hd]

    neg_q = negs[...].T
    for hd in range(N_HEADS):
        g = hd // HEADS_PER_GROUP
        w = neg_q[:, (g // 2) * LANES:(g // 2 + 1) * LANES]
        if g % 2 == 0:
            w = pltpu.roll(w, HEAD_DIM, 1)
        cur_q = qaug[hd, :, 0:LANES].astype(F32)
        qaug[hd, :, 0:LANES] = jnp.where(lo_half, cur_q, w).astype(BF16)

    def reset_state():
        for hd in range(N_HEADS):
            mst[hd] = jnp.full((tq, LANES), NEG, F32)
            acc[hd] = jnp.zeros((tq, LANES), F32)

    def finalize(branch):
        for hd in range(N_HEADS):
            a = acc[hd]
            f = gate_ref[:, 3 * hd + branch:3 * hd + branch + 1] / a[:, HEAD_DIM:HEAD_DIM + 1]
            res[hd] = res[hd] + a * f

    krel = lax.broadcasted_iota(I32, (1, tk), 1).astype(F32)
    jrow = lax.broadcasted_iota(I32, (HEAD_DIM, tk), 0)
    kcol = lax.broadcasted_iota(I32, (HEAD_DIM, tk), 1)

    def tile_aug(kt, onehot):
        k0 = kt * tk
        dk = (lax.shift_right_logical(k0, 8) - lax.shift_right_logical(q0, 8)).astype(F32)
        bias = _bias_rows(krel, dk, tk)
        if onehot:
            blk = lax.shift_right_logical(k0 + kcol, 6)
            mid = jnp.where(jrow == blk, NEG, 0.0).astype(BF16)
        else:
            mid = zeros_blk
        return jnp.concatenate([mid, bias], axis=0)

    def run_tile(kt, slot_k, onehot, masked, gpb=_GROUPS_PER_BODY):
        aug = tile_aug(kt, onehot)
        k0 = pl.multiple_of(kt * tk, tk)

        def pair_body(gp, carry):
            for gg in range(gpb):
                g = gp * gpb + gg
                r0 = pl.multiple_of(g * HEAD_DIM, HEAD_DIM)
                k_t = kvt_ref[pl.ds(slot_k * D_KV + r0, HEAD_DIM), pl.ds(k0, tk)]
                v_t = kvt_ref[pl.ds((slot_k + 1) * D_KV + r0, HEAD_DIM), pl.ds(k0, tk)]
                k_aug = jnp.concatenate([k_t, aug], axis=0)
                v_aug = jnp.concatenate([v_t, ones_blk[...]], axis=0)
                for h in range(HEADS_PER_GROUP):
                    hd = g * HEADS_PER_GROUP + h
                    s = _dot(qaug[hd], k_aug)
                    if masked is not None:
                        s = jnp.where(masked, NEG, s)
                    _online_update(hd, s, v_aug, acc, mst, nrep)
            return carry

        lax.fori_loop(0, N_KV_GROUPS // gpb, pair_body, 0)

    reset_state()
    lax.fori_loop(0, i, lambda kt, c: (run_tile(kt, 0, True, None, gpb=N_KV_GROUPS), c)[1], 0)
    run_tile(i, 0, True, above)
    finalize(1)

    reset_state()

    def win_tile(kt, mode):
        run_tile(kt, 2, False, {"diag": above, "far": below, "mid": None}[mode])

    nback = WINDOW // tk

    @pl.when(i >= nback)
    def _():
        win_tile(i - nback, "far")

    for back in range(nback - 1, 0, -1):
        @pl.when(i >= back)
        def _():
            win_tile(i - back, "mid")

    win_tile(i, "diag")
    finalize(2)

    for w in range(N_HEADS // 2):
        ow = jnp.where(lo_half, res[2 * w], pltpu.roll(res[2 * w + 1], HEAD_DIM, 1))
        osc[:, w * LANES:(w + 1) * LANES] = (ow * zg_ref[:, w * LANES:(w + 1) * LANES]).astype(BF16)
    xn = x_ref[...] + _dot(osc[...], wout_ref[...])
    if final:
        xn = _rms(xn, fg_ref[...])
    o_ref[...] = xn


def nsa_prompt_attn(q, kvtb, cmpt, gates, zg, x, wout_bf, fg, *, final, tq=256):
    b, t, _ = q.shape
    ncmp = cmpt.shape[3]
    blk = lambda bi, i: (bi, i, 0)
    return pl.pallas_call(
        functools.partial(_nsa_prompt_body, tq=tq, final=final),
        grid=(b, t // tq),
        in_specs=[pl.BlockSpec((None, tq, D_MODEL), blk),
                  pl.BlockSpec((None, 4 * D_KV, t), lambda bi, i: (bi, 0, 0)),
                  pl.BlockSpec((None, 2, D_KV, ncmp), lambda bi, i: (bi, 0, 0, 0)),
                  pl.BlockSpec((None, tq, LANES), blk),
                  pl.BlockSpec((None, tq, D_MODEL), blk),
                  pl.BlockSpec((None, tq, D_MODEL), blk),
                  pl.BlockSpec((D_MODEL, D_MODEL), lambda bi, i: (0, 0)),
                  pl.BlockSpec((1, D_MODEL), lambda bi, i: (0, 0))],
        out_specs=pl.BlockSpec((None, tq, D_MODEL), blk),
        out_shape=jax.ShapeDtypeStruct((b, t, D_MODEL), F32),
        scratch_shapes=[pltpu.VMEM((N_HEADS, tq, 2 * LANES), BF16),
                        pltpu.VMEM((N_HEADS, tq, LANES), F32),
                        pltpu.VMEM((N_HEADS, tq, LANES), F32),
                        pltpu.VMEM((N_HEADS, tq, LANES), F32),
                        pltpu.VMEM((tq, D_MODEL), BF16),
                        pltpu.VMEM((64, tq), F32),
                        pltpu.VMEM((N_KV_GROUPS * 64, tq), F32),
                        pltpu.VMEM((HEAD_DIM, tq), BF16)],
        compiler_params=_params("arbitrary", "arbitrary"),
        name="nsa_prompt_attn",
    )(q, kvtb, cmpt, gates, zg, x, wout_bf, fg)


def _nsa_sample_body(pt_ref, q_ref, kvn_ref, gate_ref, zg_ref, win_ref, w1q_ref, w2h_ref, c_ref,
                     cache_ref, o_ref, pbuf, sem, cmplo, cmphi, ebig, bias_c, bias_s, bias_w,
                     *, layer, npages):
    b = pl.program_id(0)
    nb = pl.num_programs(0)
    past = npages * PAGE_SIZE
    nchunk = past // CMP_STRIDE
    nsel = past + LANES
    nwin = WINDOW + LANES
    rows = N_HEADS * SPAD

    def page_copy(seq, p, slot):
        return pltpu.make_async_copy(cache_ref.at[layer, pt_ref[seq, p]],
                                     pbuf.at[slot, :, :, pl.ds(p * PAGE_SIZE, PAGE_SIZE)],
                                     sem.at[slot])

    def start_seq(seq, slot):
        for p in range(npages):
            page_copy(seq, p, slot).start()

    slot = lax.rem(b, 2)

    @pl.when(b == 0)
    def _():
        start_seq(0, 0)
        r = lax.broadcasted_iota(I32, (rows, 1), 0)
        slope = jnp.exp2(-0.5 * (lax.shift_right_logical(r, 3) + 1).astype(F32)) * LOG2E
        qpos = past + jnp.bitwise_and(r, SPAD - 1)
        ccol = lax.broadcasted_iota(I32, (rows, nchunk), 1)
        c_end = CMP_STRIDE * ccol + CMP_BLOCK - 1
        bias_c[...] = jnp.where(ccol < nchunk - 1, -slope * (qpos - c_end).astype(F32), NEG)
        kpos = lax.broadcasted_iota(I32, (rows, nsel), 1)
        bias_s[...] = jnp.where(kpos <= qpos, -slope * (qpos - kpos).astype(F32), NEG)
        wcol = lax.broadcasted_iota(I32, (rows, nwin), 1)
        wpos = jnp.where(wcol < WINDOW, past - WINDOW + wcol, past + wcol - WINDOW)
        dist_w = qpos - wpos
        ok = (dist_w >= 0) & (dist_w <= WINDOW) & (wcol < WINDOW + SPAD)
        bias_w[...] = jnp.where(ok, -slope * dist_w.astype(F32), NEG)
        ej = lax.broadcasted_iota(I32, (64, nsel), 0)
        ek = lax.shift_right_logical(lax.broadcasted_iota(I32, (64, nsel), 1), 6)
        ebig[...] = jnp.where(ej == ek, NEG, 0.0).astype(BF16)

    @pl.when(b + 1 < nb)
    def _():
        start_seq(b + 1, 1 - slot)

    for p in range(npages):
        page_copy(b, p, slot).wait()

    cmp_out = []
    for w in range(4):
        s, hf = divmod(w, 2)
        tok = pbuf[slot, s, hf * LANES:(hf + 1) * LANES, :].T
        tok3 = tok.reshape(nchunk, CMP_STRIDE, LANES)
        cmplo[w] = tok3[:, 0:8, :].reshape(nchunk * 8, LANES)
        cmphi[w] = tok3[:, 8:16, :].reshape(nchunk * 8, LANES)

        def load_x(l, w=w):
            src = cmplo if l < 8 else cmphi
            return src[w, pl.ds(l % 8, nchunk, stride=8), :]

        hids = _compress_hidden(load_x, w1q_ref.at[s], c_ref[s, 0:1, :], nchunk)
        out = _dot(hids[0], w2h_ref[s, 0]) + _dot(hids[1], w2h_ref[s, 1])
        rix = lax.broadcasted_iota(I32, (nchunk, LANES), 0)
        cmp_out.append(jnp.where(rix < nchunk - 1, out, 0.0))
    kc = jnp.concatenate(cmp_out[0:2], axis=1).astype(BF16)
    vc = jnp.concatenate(cmp_out[2:4], axis=1).astype(BF16)

    qf = q_ref[...].astype(F32)
    lane4 = lax.broadcasted_iota(I32, (SPAD, D_KV), 1)
    pieces = []
    for hd in range(N_HEADS):
        g, h = divmod(hd, HEADS_PER_GROUP)
        w = qf[:, g * D_KV:(g + 1) * D_KV]
        sh = ((g - h) * HEAD_DIM) % D_KV
        if sh:
            w = pltpu.roll(w, sh, 1)
        inside = (lane4 >= g * HEAD_DIM) & (lane4 < (g + 1) * HEAD_DIM)
        pieces.append(jnp.where(inside, w, 0.0))
    qbd = jnp.concatenate(pieces, axis=0).astype(BF16)

    def new_rows(c):
        new = kvn_ref[:, c * D_KV:(c + 1) * D_KV].astype(F32)
        return jnp.concatenate([new, jnp.zeros((LANES - SPAD, D_KV), F32)], axis=0).astype(BF16)

    s_c = _dot_nt(qbd, kc) + bias_c[...]
    m_c = jnp.max(s_c, axis=-1, keepdims=True)
    e_c = jnp.exp2(s_c - m_c)
    p_c = e_c / jnp.maximum(jnp.sum(e_c, axis=-1, keepdims=True), 1e-30)
    o_c = _dot(p_c.astype(BF16), vc)

    psum = jnp.concatenate(
        [sum(p_c[(g * HEADS_PER_GROUP + h) * SPAD:(g * HEADS_PER_GROUP + h + 1) * SPAD]
             for h in range(HEADS_PER_GROUP)) for g in range(N_KV_GROUPS)], axis=0)
    cj = lax.broadcasted_iota(I32, (nchunk, 64), 0)
    jj = lax.broadcasted_iota(I32, (nchunk, 64), 1)
    ov = jnp.where(CMP_STRIDE * cj < SEL_BLOCK * (jj + 1),
                   jnp.where(CMP_STRIDE * cj + CMP_BLOCK - 1 >= SEL_BLOCK * jj, 1.0, 0.0),
                   0.0).astype(BF16)
    p_hi = psum.astype(BF16)
    p_lo = (psum - p_hi.astype(F32)).astype(BF16)
    imp = _dot(p_hi, ov) + _dot(p_lo, ov)
    nrow = N_KV_GROUPS * SPAD
    jb = lax.broadcasted_iota(I32, (nrow, 64), 1)
    cur = lax.shift_right_logical(
        past + jnp.bitwise_and(lax.broadcasted_iota(I32, (nrow, 64), 0), SPAD - 1), 6)
    elig = jb <= cur
    forced = (jb == 0) | (jb == cur) | (jb == cur - 1)
    sc = jnp.where(elig, jnp.where(forced, FORCE_SCORE, imp), NEG)
    rank = jnp.zeros((nrow, 64), I32)
    for ii in range(past // SEL_BLOCK + 1):
        col = sc[:, ii:ii + 1]
        rank = rank + jnp.where(col > sc, 1, jnp.where(col == sc, jnp.where(ii < jb, 1, 0), 0))
    neg = jnp.where(elig, jnp.where(rank < N_SELECT, 0.0, 1.0), 1.0)
    neg_rows = jnp.concatenate(
        [neg[(hd // HEADS_PER_GROUP) * SPAD:(hd // HEADS_PER_GROUP + 1) * SPAD] for hd in range(N_HEADS)],
        axis=0).astype(BF16)

    k_new, v_new = new_rows(0), new_rows(1)
    s_s = (jnp.concatenate([_dot(qbd, pbuf[slot, 2].astype(BF16)), _dot_nt(qbd, k_new)], axis=1)
           + _dot(neg_rows, ebig[...]) + bias_s[...])
    m_s = jnp.max(s_s, axis=-1, keepdims=True)
    e_s = jnp.exp2(s_s - m_s)
    e_sb = e_s.astype(BF16)
    o_s = (_dot_nt(e_sb[:, 0:past], pbuf[slot, 3].astype(BF16)) + _dot(e_sb[:, past:], v_new)
           ) / jnp.sum(e_s, axis=-1, keepdims=True)

    kw_new, vw_new = new_rows(2), new_rows(3)
    s_w = jnp.concatenate([_dot(qbd, win_ref[0].astype(BF16)), _dot_nt(qbd, kw_new)], axis=1) + bias_w[...]
    m_w = jnp.max(s_w, axis=-1, keepdims=True)
    e_w = jnp.exp2(s_w - m_w)
    e_wb = e_w.astype(BF16)
    o_w = (_dot_nt(e_wb[:, 0:WINDOW], win_ref[1].astype(BF16)) + _dot(e_wb[:, WINDOW:], vw_new)
           ) / jnp.sum(e_w, axis=-1, keepdims=True)

    gates = gate_ref[...]
    gcol = lambda br: jnp.concatenate(
        [gates[:, 3 * hd + br:3 * hd + br + 1] for hd in range(N_HEADS)], axis=0)
    o_all = gcol(0) * o_c + gcol(1) * o_s + gcol(2) * o_w

    outs = []
    for g in range(N_KV_GROUPS):
        inside = (lane4 >= g * HEAD_DIM) & (lane4 < (g + 1) * HEAD_DIM)
        tot = jnp.zeros((SPAD, D_KV), F32)
        for h in range(HEADS_PER_GROUP):
            hd = g * HEADS_PER_GROUP + h
            piece = jnp.where(inside, o_all[hd * SPAD:(hd + 1) * SPAD], 0.0)
            sh = ((h - g) * HEAD_DIM) % D_KV
            if sh:
                piece = pltpu.roll(piece, sh, 1)
            tot = tot + piece
        outs.append(tot)
    o_ref[...] = jnp.concatenate(outs, axis=1) * zg_ref[...]


def nsa_sample_attn(page_table, q, kvn, gates, zg, win_t, w1q, w2h, const, cache_t, *, layer):
    nb, npages = page_table.shape
    past = npages * PAGE_SIZE
    nchunk = past // CMP_STRIDE
    rows = N_HEADS * SPAD
    seq = lambda i, pt: (i, 0)
    grid_spec = pltpu.PrefetchScalarGridSpec(
        num_scalar_prefetch=1,
        grid=(nb,),
        in_specs=[pl.BlockSpec((SPAD, D_MODEL), seq),
                  pl.BlockSpec((SPAD, 4 * D_KV), seq),
                  pl.BlockSpec((SPAD, LANES), seq),
                  pl.BlockSpec((SPAD, D_MODEL), seq),
                  pl.BlockSpec((None, None, 2, D_KV, WINDOW), lambda i, pt: (layer, i, 0, 0, 0)),
                  pl.BlockSpec((2, 4, 256, 256), lambda i, pt: (0, 0, 0, 0)),
                  pl.BlockSpec((2, 2, CMP_HIDDEN, LANES), lambda i, pt: (0, 0, 0, 0)),
                  pl.BlockSpec((2, 8, CMP_HIDDEN), lambda i, pt: (0, 0, 0)),
                  pl.BlockSpec(memory_space=pl.ANY)],
        out_specs=pl.BlockSpec((SPAD, D_MODEL), seq),
        scratch_shapes=[pltpu.VMEM((2, 4, D_KV, past), F32),
                        pltpu.SemaphoreType.DMA((2,)),
                        pltpu.VMEM((4, nchunk * 8, LANES), F32),
                        pltpu.VMEM((4, nchunk * 8, LANES), F32),
                        pltpu.VMEM((64, past + LANES), BF16),
                        pltpu.VMEM((rows, nchunk), F32),
                        pltpu.VMEM((rows, past + LANES), F32),
                        pltpu.VMEM((rows, WINDOW + LANES), F32)])
    return pl.pallas_call(
        functools.partial(_nsa_sample_body, layer=layer, npages=npages),
        grid_spec=grid_spec,
        out_shape=jax.ShapeDtypeStruct((nb * SPAD, D_MODEL), F32),
        compiler_params=_params("arbitrary"),
        name="nsa_sample_attn",
    )(page_table, q, kvn, gates, zg, win_t, w1q, w2h, const, cache_t)


def _outproj_body(o_ref, x_ref, w_ref, fg_ref, y_ref, *, final):
    xn = x_ref[...] + _dot(o_ref[...].astype(BF16), w_ref[...])
    if final:
        xn = _rms(xn, fg_ref[...])
    y_ref[...] = xn


def outproj(o, x, w_bf, fg, *, final, tm=256):
    n = o.shape[0]
    row = lambda i: (i, 0)
    return pl.pallas_call(
        functools.partial(_outproj_body, final=final),
        grid=(n // tm,),
        in_specs=[pl.BlockSpec((tm, D_MODEL), row), pl.BlockSpec((tm, D_MODEL), row),
                  pl.BlockSpec((D_MODEL, D_MODEL), lambda i: (0, 0)),
                  pl.BlockSpec((1, D_MODEL), lambda i: (0, 0))],
        out_specs=pl.BlockSpec((tm, D_MODEL), row),
        out_shape=jax.ShapeDtypeStruct((n, D_MODEL), F32),
        compiler_params=_params("arbitrary"),
        name="outproj",
    )(o, x, w_bf, fg)


def _prep_cmp_weights(pe, w1, w2):
    top = w1[:, :CMP_STRIDE].reshape(2, 4, 4 * HEAD_DIM, CMP_HIDDEN)
    bot = w1[:, CMP_STRIDE:].reshape(2, 4, 4 * HEAD_DIM, CMP_HIDDEN)
    w1q = jnp.concatenate([top, bot], axis=-1).astype(BF16)
    w2h = jnp.zeros((2, 2, CMP_HIDDEN, LANES), F32)
    for hf in range(2):
        w2h = w2h.at[:, hf, :, hf * HEAD_DIM:(hf + 1) * HEAD_DIM].set(w2)
    w2t = jnp.swapaxes(w2h, 2, 3)
    pe_pad = jnp.zeros((2, 8, CMP_BLOCK * HEAD_DIM), F32).at[:, 0].set(pe.reshape(2, -1))
    w1_flat = w1.reshape(2, CMP_BLOCK * HEAD_DIM, CMP_HIDDEN)
    return w1q, w2h.astype(BF16), w2t.astype(BF16), pe_pad.astype(BF16), w1_flat.astype(BF16)


def _prep_nsa_in_weights(w_in):
    gl = jnp.pad(w_in[:, 3584:], ((0, 0), (0, LANES - N_BRANCH * N_HEADS)))
    w_a = jnp.concatenate([w_in[:, 0:2048], w_in[:, 2048:2560], gl], axis=1).astype(BF16)
    w_t = w_in[:, 2048:3584].T.astype(BF16)
    w_n = w_in[:, 2560:3584].astype(BF16)
    return w_a, w_t, w_n


def kernel(x_prompt, x_sample, cache_kv, state_kv_win, state_conv, page_table, norm_g, final_norm_g,
           conv_w_in, conv_dw_w, conv_dw_b, conv_ln_g, conv_ln_b, conv_w_out,
           nsa_w_in, nsa_w_out, nsa_cmp_pe, nsa_cmp_w1, nsa_cmp_w2):
    bsz, seq, _ = x_prompt.shape
    nb = x_sample.shape[0]
    n_nsa = nsa_w_in.shape[0]
    n_pool = cache_kv.shape[1]
    win_keep = state_kv_win.shape[2]
    ns = SPAD * nb
    xp = x_prompt
    xs = jnp.pad(jnp.swapaxes(x_sample, 0, 1), ((0, SPAD - DEC_SEQ), (0, 0), (0, 0)))
    cache_t = jnp.transpose(cache_kv, (0, 1, 3, 4, 5, 2)).reshape(n_nsa, n_pool, 4, D_KV, PAGE_SIZE)
    win_state_t = jnp.transpose(state_kv_win, (0, 1, 3, 4, 5, 2)).reshape(n_nsa, nb, 2, D_KV, win_keep)
    conv_state_t = jnp.swapaxes(state_conv, 1, 2)
    fg = final_norm_g.reshape(1, D_MODEL)

    kv_p, kv_s, win_p, win_s, conv_p, conv_s = [], [], [], [], [], []
    for layer in range(DEPTH):
        j = layer // 2
        g = norm_g[layer].reshape(1, D_MODEL)
        last = layer == DEPTH - 1
        if layer % 2 == 0:
            w_in = conv_w_in[j].astype(BF16)
            w_out = conv_w_out[j].astype(BF16)
            db = conv_dw_b[j].reshape(1, D_CONV)
            lg = conv_ln_g[j].reshape(1, D_CONV)
            lb = conv_ln_b[j].reshape(1, D_CONV)
            v, zg = conv_inproj(xp.reshape(bsz * seq, D_MODEL), g, w_in)
            v = v.reshape(bsz, seq, D_CONV)
            xp = conv_prompt(v, zg.reshape(bsz, seq, D_CONV), xp, conv_dw_w[j], db, lg, lb, w_out)
            conv_p.append(v[:, seq - CONV_STATE:])
            vs, zgs = conv_inproj(xs.reshape(ns, D_MODEL), g, w_in)
            vs = vs.reshape(SPAD, nb, D_CONV)
            xs = conv_sample(conv_state_t, vs, zgs.reshape(SPAD, nb, D_CONV), xs, conv_dw_w[j],
                             db, lg, lb, w_out, layer=j)
            conv_s.append(vs[:DEC_SEQ])
        else:
            w_a, w_t, w_n = _prep_nsa_in_weights(nsa_w_in[j])
            w_out = nsa_w_out[j].astype(BF16)
            w1q, w2h, w2t, pe_pad, w1_flat = _prep_cmp_weights(nsa_cmp_pe[j], nsa_cmp_w1[j], nsa_cmp_w2[j])
            const = cmp_const(pe_pad, w1_flat)
            q, zg, gates, kvpt, kvwt, kcn, kvtb = nsa_inproj(xp, g, w_a, w_t)
            cmpt = compress_prompt(kcn, w1q, w2t, const)
            xp = nsa_prompt_attn(q, kvtb, cmpt, gates, zg, xp, w_out, fg, final=last)
            kv_p.append(jnp.transpose(kvpt.reshape(bsz, 4, N_KV_GROUPS, HEAD_DIM, seq), (0, 4, 1, 2, 3)))
            wtail = kvwt.reshape(bsz, 2, N_KV_GROUPS, HEAD_DIM, seq)[..., seq - min(WINDOW, seq):]
            win_p.append(jnp.transpose(wtail, (0, 4, 1, 2, 3)))
            xq = jnp.swapaxes(xs, 0, 1).reshape(ns, D_MODEL)
            qs, zgs, gs, kvpts, kvwts, kvns = nsa_inproj(xq.reshape(1, ns, D_MODEL), g, w_a, w_t, w_n)
            og = nsa_sample_attn(page_table, qs.reshape(ns, D_MODEL), kvns.reshape(ns, 4 * D_KV),
                                 gs.reshape(ns, LANES), zgs.reshape(ns, D_MODEL),
                                 win_state_t, w1q, w2h, const, cache_t, layer=j)
            xq = outproj(og, xq, w_out, fg, final=last)
            xs = jnp.swapaxes(xq.reshape(nb, SPAD, D_MODEL), 0, 1)
            kvps5 = kvpts.reshape(4, N_KV_GROUPS, HEAD_DIM, nb, SPAD)[..., :DEC_SEQ]
            kv_s.append(jnp.transpose(kvps5, (3, 4, 0, 1, 2)))
            kvws5 = kvwts.reshape(2, N_KV_GROUPS, HEAD_DIM, nb, SPAD)[..., :DEC_SEQ]
            new_w = jnp.transpose(kvws5, (3, 4, 0, 1, 2))
            win_s.append(new_w)
    y_sample = jnp.swapaxes(xs[:DEC_SEQ], 0, 1)
    win_sample = jnp.concatenate([state_kv_win, jnp.stack(win_s)], axis=2)[:, :, -win_keep:]
    conv_sample_out = jnp.swapaxes(
        jnp.concatenate([conv_state_t[:, DEC_SEQ:], jnp.stack(conv_s)], axis=1), 1, 2)
    return (xp, y_sample, jnp.stack(kv_p), jnp.stack(kv_s), jnp.stack(win_p), win_sample,
            jnp.stack(conv_p), conv_sample_out)
```

```python
import functools
import math

import jax
import jax.numpy as jnp
import ml_dtypes
import numpy as np
from jax import lax
from jax.experimental import pallas as pl
from jax.experimental.pallas import tpu as pltpu

F32 = jnp.float32
BF16 = jnp.bfloat16
I32 = jnp.int32

D_MODEL = 1024
DEPTH = 4
DEC_SEQ = 4
PAGE_SIZE = 128
D_CONV = 2048
CONV_WIDTH = 31
CONV_STATE = CONV_WIDTH - 1
N_HEADS = 16
HEAD_DIM = 64
N_KV_GROUPS = 4
HEADS_PER_GROUP = 4
D_KV = N_KV_GROUPS * HEAD_DIM
CMP_BLOCK = 32
CMP_STRIDE = 16
CMP_HIDDEN = 128
SEL_BLOCK = 64
N_SELECT = 16
WINDOW = 512
N_BRANCH = 3
NORM_EPS = 1e-6
FORCE_SCORE = 1e4
NEG = -1e30
LOG2E = math.log2(math.e)
SPAD = 8
LANES = 128
VMEM_LIMIT = 56 * 1024 * 1024


def _params(*sem):
    return pltpu.CompilerParams(dimension_semantics=sem, vmem_limit_bytes=VMEM_LIMIT)


def _rms(x, g):
    return x * lax.rsqrt(jnp.mean(x * x, axis=-1, keepdims=True) + NORM_EPS) * g


def _silu(x):
    return x * jax.nn.sigmoid(x)


def _dot(a, b):
    return jnp.dot(a, b, preferred_element_type=F32)


def _dot_nt(a, b):
    return lax.dot_general(a, b, (((1,), (1,)), ((), ())), preferred_element_type=F32)


def _rep(x, n):
    return jnp.concatenate([x] * n, axis=1) if n > 1 else x


def _slope2(hd):
    return 2.0 ** (-(hd + 1) / 2.0) * LOG2E


def _hi_lo(x):
    hi = float(np.float32(x).astype(ml_dtypes.bfloat16).astype(np.float32))
    lo = float(np.float32(x - hi).astype(ml_dtypes.bfloat16).astype(np.float32))
    return hi, lo


def _conv_inproj_body(x_ref, g_ref, w_ref, v_ref, zg_ref):
    h = _rms(x_ref[...], g_ref[...]).astype(BF16)
    ch = 512
    for c in range(D_CONV // ch):
        lo = c * ch
        a = _dot(h, w_ref[:, lo:lo + ch])
        gl = _dot(h, w_ref[:, D_CONV + lo:D_CONV + lo + ch])
        z = _dot(h, w_ref[:, 2 * D_CONV + lo:2 * D_CONV + lo + ch])
        v_ref[:, lo:lo + ch] = a * jax.nn.sigmoid(gl)
        zg_ref[:, lo:lo + ch] = _silu(z)


def conv_inproj(x, g, w_bf, tm=256):
    n = x.shape[0]
    return pl.pallas_call(
        _conv_inproj_body,
        grid=(n // tm,),
        in_specs=[pl.BlockSpec((tm, D_MODEL), lambda i: (i, 0)),
                  pl.BlockSpec((1, D_MODEL), lambda i: (0, 0)),
                  pl.BlockSpec((D_MODEL, 3 * D_CONV), lambda i: (0, 0))],
        out_specs=[pl.BlockSpec((tm, D_CONV), lambda i: (i, 0)),
                   pl.BlockSpec((tm, D_CONV), lambda i: (i, 0))],
        out_shape=[jax.ShapeDtypeStruct((n, D_CONV), F32),
                   jax.ShapeDtypeStruct((n, D_CONV), F32)],
        compiler_params=_params("arbitrary"),
        name="conv_inproj",
    )(x, g, w_bf)


_HDR = 32
_RW = 64
_PH = 4
_NSLAB = D_CONV // LANES
_GROUPS_PER_BODY = 2


def _ln_gate(c, lg, lb, zg):
    mu = jnp.mean(c, axis=-1, keepdims=True)
    xc = c - mu
    var = jnp.mean(xc * xc, axis=-1, keepdims=True)
    y = xc * lax.rsqrt(var + NORM_EPS) * lg + lb
    return _silu(y) * zg


def _conv_prompt_body(v_ref, zg_ref, x_ref, dw_ref, db_ref, lg_ref, lb_ref, wout_ref, o_ref,
                      vbuf, cbuf, ybuf, *, tt):
    i = pl.program_id(1)

    @pl.when(i == 0)
    def _():
        vbuf[:, 0:_HDR, :] = jnp.zeros((_NSLAB, _HDR, LANES), F32)

    @pl.when(i > 0)
    def _():
        vbuf[:, 0:_HDR, :] = vbuf[:, tt:tt + _HDR, :]

    for w in range(_NSLAB):
        vbuf[w, _HDR:_HDR + tt, :] = v_ref[:, w * LANES:(w + 1) * LANES]
    off = _HDR - CONV_STATE

    def row_body(r, carry):
        r0 = pl.multiple_of(r * _RW, _RW)

        def slab_body(w, carry2):
            c0 = pl.multiple_of(w * LANES, LANES)
            starts = [r0 + sub * 8 * _PH + u for sub in range(_RW // (8 * _PH)) for u in range(_PH)]
            accs = [[None, None] for _ in starts]
            for j in range(CONV_WIDTH):
                wj = dw_ref[pl.ds(j, 1), pl.ds(c0, LANES)]
                for a, st in enumerate(starts):
                    term = vbuf[w, pl.ds(st + j + off, 8, stride=_PH), :] * wj
                    prev = accs[a][j % 2]
                    accs[a][j % 2] = term if prev is None else prev + term
            bias = db_ref[:, pl.ds(c0, LANES)]
            for a, st in enumerate(starts):
                cbuf[w, pl.ds(st, 8, stride=_PH), :] = (accs[a][0] + bias) + accs[a][1]
            return carry2

        lax.fori_loop(0, _NSLAB, slab_body, 0)
        c = jnp.concatenate([cbuf[w, pl.ds(r0, _RW), :] for w in range(_NSLAB)], axis=1)
        y = _ln_gate(c, lg_ref[...], lb_ref[...], zg_ref[pl.ds(r0, _RW), :])
        ybuf[pl.ds(r0, _RW), :] = y.astype(BF16)
        return carry

    lax.fori_loop(0, tt // _RW, row_body, 0)
    o_ref[...] = x_ref[...] + _dot(ybuf[...], wout_ref[...])


def conv_prompt(v, zg, x, dw, db, lg, lb, wout_bf, tt=512):
    b, t, _ = v.shape
    return pl.pallas_call(
        functools.partial(_conv_prompt_body, tt=tt),
        grid=(b, t // tt),
        in_specs=[pl.BlockSpec((None, tt, D_CONV), lambda bi, i: (bi, i, 0)),
                  pl.BlockSpec((None, tt, D_CONV), lambda bi, i: (bi, i, 0)),
                  pl.BlockSpec((None, tt, D_MODEL), lambda bi, i: (bi, i, 0)),
                  pl.BlockSpec((CONV_WIDTH, D_CONV), lambda bi, i: (0, 0)),
                  pl.BlockSpec((1, D_CONV), lambda bi, i: (0, 0)),
                  pl.BlockSpec((1, D_CONV), lambda bi, i: (0, 0)),
                  pl.BlockSpec((1, D_CONV), lambda bi, i: (0, 0)),
                  pl.BlockSpec((D_CONV, D_MODEL), lambda bi, i: (0, 0))],
        out_specs=pl.BlockSpec((None, tt, D_MODEL), lambda bi, i: (bi, i, 0)),
        out_shape=jax.ShapeDtypeStruct((b, t, D_MODEL), F32),
        scratch_shapes=[pltpu.VMEM((_NSLAB, tt + _HDR, LANES), F32),
                        pltpu.VMEM((_NSLAB, tt, LANES), F32),
                        pltpu.VMEM((tt, D_CONV), BF16)],
        compiler_params=_params("arbitrary", "arbitrary"),
        name="conv_prompt",
    )(v, zg, x, dw, db, lg, lb, wout_bf)


def _conv_sample_body(st_ref, v_ref, zg_ref, x_ref, dw_ref, db_ref, lg_ref, lb_ref, wout_ref,
                      o_ref, ybuf, *, bs):
    ch = 512
    for t in range(DEC_SEQ):
        parts = []
        for c in range(D_CONV // ch):
            cs = slice(c * ch, (c + 1) * ch)
            acc = jnp.zeros((bs, ch), F32)
            for j in range(CONV_WIDTH):
                r = t + j
                src = st_ref[r, :, cs] if r < CONV_STATE else v_ref[r - CONV_STATE, :, cs]
                acc = acc + src * dw_ref[j:j + 1, cs]
            parts.append(acc)
        c_t = jnp.concatenate(parts, axis=1) + db_ref[...]
        ybuf[t * bs:(t + 1) * bs, :] = _ln_gate(c_t, lg_ref[...], lb_ref[...], zg_ref[t])
    y = _dot(ybuf[...].astype(BF16), wout_ref[...])
    for t in range(DEC_SEQ):
        o_ref[t] = x_ref[t] + y[t * bs:(t + 1) * bs]
    for t in range(DEC_SEQ, SPAD):
        o_ref[t] = jnp.zeros((bs, D_MODEL), F32)


def conv_sample(state_t, v, zg, x, dw, db, lg, lb, wout_bf, *, layer, bs=16):
    nb = state_t.shape[2]
    blk = lambda i: (0, i, 0)
    return pl.pallas_call(
        functools.partial(_conv_sample_body, bs=bs),
        grid=(nb // bs,),
        in_specs=[pl.BlockSpec((None, CONV_STATE, bs, D_CONV), lambda i: (layer, 0, i, 0)),
                  pl.BlockSpec((DEC_SEQ, bs, D_CONV), blk),
                  pl.BlockSpec((DEC_SEQ, bs, D_CONV), blk),
                  pl.BlockSpec((SPAD, bs, D_MODEL), blk),
                  pl.BlockSpec((CONV_WIDTH, D_CONV), lambda i: (0, 0)),
                  pl.BlockSpec((1, D_CONV), lambda i: (0, 0)),
                  pl.BlockSpec((1, D_CONV), lambda i: (0, 0)),
                  pl.BlockSpec((1, D_CONV), lambda i: (0, 0)),
                  pl.BlockSpec((D_CONV, D_MODEL), lambda i: (0, 0))],
        out_specs=pl.BlockSpec((SPAD, bs, D_MODEL), blk),
        out_shape=jax.ShapeDtypeStruct((SPAD, nb, D_MODEL), F32),
        scratch_shapes=[pltpu.VMEM((DEC_SEQ * bs, D_CONV), F32)],
        compiler_params=_params("arbitrary"),
        name="conv_sample",
    )(state_t, v, zg, x, dw, db, lg, lb, wout_bf)


def _nsa_inproj_body(*refs, sample):
    if sample:
        x_ref, g_ref, wa_ref, wt_ref, wn_ref, q_ref, zg_ref, gate_ref, kvp_ref, kvw_ref, kvn_ref = refs
    else:
        x_ref, g_ref, wa_ref, wt_ref, q_ref, zg_ref, gate_ref, kvp_ref, kvw_ref, kcn_ref, kvtb_ref = refs
    h = _rms(x_ref[...], g_ref[...]).astype(BF16)
    ch = 512
    for c in range(2):
        q = _dot(h, wa_ref[:, c * ch:(c + 1) * ch])
        q_ref[:, c * ch:(c + 1) * ch] = (q * (HEAD_DIM ** -0.5 * LOG2E)).astype(BF16)
    for c in range(2):
        z = _dot(h, wa_ref[:, D_MODEL + c * ch:D_MODEL + (c + 1) * ch])
        zg_ref[:, c * ch:(c + 1) * ch] = _silu(z)
    gate_ref[...] = jax.nn.sigmoid(_dot(h, wa_ref[:, 2560:2688]))
    for c in range(3):
        kvt = _dot_nt(wt_ref[c * ch:(c + 1) * ch, :], h)
        if c < 2:
            kvp_ref[c * ch:(c + 1) * ch, :] = kvt
        else:
            kvw_ref[...] = kvt
        if not sample and c >= 1:
            kvtb_ref[(c - 1) * ch:c * ch, :] = kvt.astype(BF16)
    if sample:
        for c in range(2):
            kvn_ref[:, c * ch:(c + 1) * ch] = _dot(h, wn_ref[:, c * ch:(c + 1) * ch]).astype(BF16)
    else:
        kcn_ref[...] = _dot(h, wa_ref[:, 2048:2560])


def nsa_inproj(x, g, w_a, w_t, w_n=None, tm=256):
    b, t, _ = x.shape
    sample = w_n is not None
    tok = lambda bi, i: (bi, i, 0)
    const = lambda bi, i: (0, 0)
    in_specs = [pl.BlockSpec((None, tm, D_MODEL), tok),
                pl.BlockSpec((1, D_MODEL), const),
                pl.BlockSpec((D_MODEL, 2688), const),
                pl.BlockSpec((6 * D_KV, D_MODEL), const)]
    out_specs = [pl.BlockSpec((None, tm, D_MODEL), tok),
                 pl.BlockSpec((None, tm, D_MODEL), tok),
                 pl.BlockSpec((None, tm, LANES), tok),
                 pl.BlockSpec((None, 4 * D_KV, tm), lambda bi, i: (bi, 0, i)),
                 pl.BlockSpec((None, 2 * D_KV, tm), lambda bi, i: (bi, 0, i))]
    out_shape = [jax.ShapeDtypeStruct((b, t, D_MODEL), BF16),
                 jax.ShapeDtypeStruct((b, t, D_MODEL), F32),
                 jax.ShapeDtypeStruct((b, t, LANES), F32),
                 jax.ShapeDtypeStruct((b, 4 * D_KV, t), F32),
                 jax.ShapeDtypeStruct((b, 2 * D_KV, t), F32)]
    args = [x, g, w_a, w_t]
    if sample:
        in_specs.append(pl.BlockSpec((D_MODEL, 4 * D_KV), const))
        out_specs.append(pl.BlockSpec((None, tm, 4 * D_KV), tok))
        out_shape.append(jax.ShapeDtypeStruct((b, t, 4 * D_KV), BF16))
        args.append(w_n)
    else:
        out_specs.append(pl.BlockSpec((None, tm, 2 * D_KV), tok))
        out_shape.append(jax.ShapeDtypeStruct((b, t, 2 * D_KV), F32))
        out_specs.append(pl.BlockSpec((None, 4 * D_KV, tm), lambda bi, i: (bi, 0, i)))
        out_shape.append(jax.ShapeDtypeStruct((b, 4 * D_KV, t), BF16))
    return pl.pallas_call(
        functools.partial(_nsa_inproj_body, sample=sample),
        grid=(b, t // tm),
        in_specs=in_specs, out_specs=out_specs, out_shape=out_shape,
        compiler_params=_params("arbitrary", "arbitrary"),
        name="nsa_inproj_sample" if sample else "nsa_inproj",
    )(*args)


def _cmp_const_body(pe_ref, w1_ref, o_ref):
    o_ref[...] = _dot(pe_ref[...], w1_ref[...])


def cmp_const(pe_pad_bf, w1_flat_bf):
    return pl.pallas_call(
        _cmp_const_body,
        grid=(2,),
        in_specs=[pl.BlockSpec((None, 8, 2048), lambda s: (s, 0, 0)),
                  pl.BlockSpec((None, 2048, CMP_HIDDEN), lambda s: (s, 0, 0))],
        out_specs=pl.BlockSpec((None, 8, CMP_HIDDEN), lambda s: (s, 0, 0)),
        out_shape=jax.ShapeDtypeStruct((2, 8, CMP_HIDDEN), F32),
        compiler_params=_params("arbitrary"),
        name="cmp_const",
    )(pe_pad_bf, w1_flat_bf)


def _compress_hidden(load_x, w1q_ref, const_row, n):
    accs = [jnp.zeros((n, 2 * CMP_HIDDEN), F32) for _ in range(2)]
    for qd in range(4):
        xs = [load_x(4 * qd + j) for j in range(4)]
        w = w1q_ref[qd]
        for hf in range(2):
            lhs = jnp.concatenate([x[:, hf * HEAD_DIM:(hf + 1) * HEAD_DIM] for x in xs], axis=1)
            accs[hf] = accs[hf] + _dot(lhs.astype(BF16), w)
    hids = []
    for hf in range(2):
        first = accs[hf][:, :CMP_HIDDEN]
        second = pltpu.roll(accs[hf][:, CMP_HIDDEN:], n - 1, 0)
        hids.append(_silu(first + second + const_row).astype(BF16))
    return hids


def _compress_prompt_body(x_ref, w1q_ref, w2t_ref, c_ref, o_ref, *, n):
    load_x = lambda l: x_ref[pl.ds(l, n, stride=CMP_STRIDE), :]
    hids = _compress_hidden(load_x, w1q_ref, c_ref[0:1, :], n)
    out_t = _dot_nt(w2t_ref[0], hids[0]) + _dot_nt(w2t_ref[1], hids[1])
    cols = lax.broadcasted_iota(I32, (LANES, n), 1)
    o_ref[...] = jnp.where(cols < n - 1, out_t, 0.0).astype(BF16)


def compress_prompt(kcn, w1q, w2t, const):
    b, t, _ = kcn.shape
    n = t // CMP_STRIDE
    return pl.pallas_call(
        functools.partial(_compress_prompt_body, n=n),
        grid=(b, 2, 2),
        in_specs=[pl.BlockSpec((None, t, LANES), lambda bi, s, w: (bi, 0, 2 * s + w)),
                  pl.BlockSpec((None, 4, 256, 256), lambda bi, s, w: (s, 0, 0, 0)),
                  pl.BlockSpec((None, 2, LANES, CMP_HIDDEN), lambda bi, s, w: (s, 0, 0, 0)),
                  pl.BlockSpec((None, 8, CMP_HIDDEN), lambda bi, s, w: (s, 0, 0))],
        out_specs=pl.BlockSpec((None, None, LANES, n), lambda bi, s, w: (bi, s, w, 0)),
        out_shape=jax.ShapeDtypeStruct((b, 2, D_KV, n), BF16),
        compiler_params=_params("arbitrary", "arbitrary", "arbitrary"),
        name="compress_prompt",
    )(kcn, w1q, w2t, const)


def _bias_rows(pos_lo, pos_hi, nkeys):
    ri = lax.broadcasted_iota(I32, (LANES, nkeys), 0)
    return jnp.where(ri < 2, pos_lo, jnp.where(ri < 4, pos_hi, 0.0)).astype(BF16)


def _online_update(hd, s, v_aug, acc, mst, nrep):
    m_old = mst[hd]
    m_new = jnp.maximum(m_old, jnp.max(s, axis=-1, keepdims=True))
    p = jnp.exp2(s - _rep(m_new, nrep))
    alpha = jnp.exp2(m_old - m_new)
    acc[hd] = alpha * acc[hd] + _dot_nt(p.astype(BF16), v_aug)
    mst[hd] = m_new


def _nsa_prompt_body(q_ref, kvt_ref, cmp_ref, gate_ref, zg_ref, x_ref, wout_ref, fg_ref,
                     o_ref, qaug, acc, res, mst, osc, negs, ones_blk, *, tq, final):
    i = pl.program_id(1)
    q0 = i * tq
    tk = tq
    nrep = tk // LANES
    lane = lax.broadcasted_iota(I32, (tq, LANES), 1)
    lo_half = lane < HEAD_DIM
    kid = lax.broadcasted_iota(I32, (tq, tk), 1)
    qid = lax.broadcasted_iota(I32, (tq, tk), 0)
    above = kid > qid
    below = kid < qid

    orow = lax.broadcasted_iota(I32, (HEAD_DIM, tk), 0)
    ones_blk[...] = jnp.where(orow == 0, 1.0, 0.0).astype(BF16)
    zeros_blk = jnp.zeros((HEAD_DIM, tk), BF16)

    for hd in range(N_HEADS):
        w = q_ref[:, (hd // 2) * LANES:(hd // 2 + 1) * LANES].astype(F32)
        if hd % 2 == 1:
            w = pltpu.roll(w, HEAD_DIM, 1)
        qaug[hd, :, 0:LANES] = jnp.where(lo_half, w, 0.0).astype(BF16)
        hi, lo = _hi_lo(_slope2(hd))
        cols = jnp.where(lane == 0, hi, jnp.where(lane == 1, lo, jnp.where(
            lane == 2, 256.0 * hi, jnp.where(lane == 3, 256.0 * lo, 0.0))))
        qaug[hd, :, LANES:2 * LANES] = cols.astype(BF16)

    ncmp = cmp_ref.shape[2]
    cid = lax.broadcasted_iota(I32, (tq, ncmp), 1)
    c_end = CMP_STRIDE * cid + CMP_BLOCK - 1
    mask_c = c_end <= q0 + lax.broadcasted_iota(I32, (tq, ncmp), 0)
    ce1 = CMP_STRIDE * lax.broadcasted_iota(I32, (1, ncmp), 1) + CMP_BLOCK - 1
    aug_c = jnp.concatenate(
        [zeros_blk[:, 0:ncmp],
         _bias_rows(jnp.bitwise_and(ce1, 255).astype(F32),
                    (lax.shift_right_logical(ce1, 8) - lax.shift_right_logical(q0, 8)).astype(F32),
                    ncmp)], axis=0)
    nsb = 64
    jt = lax.broadcasted_iota(I32, (nsb, ncmp), 0)
    ct = lax.broadcasted_iota(I32, (nsb, ncmp), 1)
    ov_t = jnp.where(CMP_STRIDE * ct < SEL_BLOCK * (jt + 1),
                     jnp.where(CMP_STRIDE * ct + CMP_BLOCK - 1 >= SEL_BLOCK * jt, 1.0, 0.0),
                     0.0).astype(BF16)
    jq = lax.broadcasted_iota(I32, (nsb, tq), 0)
    cur = lax.shift_right_logical(q0 + lax.broadcasted_iota(I32, (nsb, tq), 1), 6)
    elig = jq <= cur
    forced = (jq == 0) | (jq == cur) | (jq == cur - 1)
    n_elig = (i + 1) * (tq // SEL_BLOCK)

    def cmp_pair(gp, carry):
        scores = []
        for gg in range(_GROUPS_PER_BODY):
            g = gp * _GROUPS_PER_BODY + gg
            r0 = pl.multiple_of(g * HEAD_DIM, HEAD_DIM)
            kc_aug = jnp.concatenate([cmp_ref[0, pl.ds(r0, HEAD_DIM), :], aug_c], axis=0)
            vc_aug = jnp.concatenate([cmp_ref[1, pl.ds(r0, HEAD_DIM), :], zeros_blk[:, 0:ncmp]], axis=0)
            psum = jnp.zeros((tq, ncmp), F32)
            for h in range(HEADS_PER_GROUP):
                hd = g * HEADS_PER_GROUP + h
                s = jnp.where(mask_c, _dot(qaug[hd], kc_aug), NEG)
                m = jnp.max(s, axis=-1, keepdims=True)
                e = jnp.where(mask_c, jnp.exp2(s - m), 0.0)
                p = e / jnp.maximum(jnp.sum(e, axis=-1, keepdims=True), 1e-30)
                psum = psum + p
                acc[hd] = _dot_nt(p.astype(BF16), vc_aug)
            p_hi = psum.astype(BF16)
            p_lo = (psum - p_hi.astype(F32)).astype(BF16)
            imp_t = _dot_nt(ov_t, p_hi) + _dot_nt(ov_t, p_lo)
            sc = jnp.where(elig, jnp.where(forced, FORCE_SCORE, imp_t), NEG)
            negs[pl.ds(pl.multiple_of(g * nsb, nsb), nsb), :] = sc
            scores.append(sc)

        def rank_body(ii, ranks):
            out = []
            for gg in range(_GROUPS_PER_BODY):
                g = gp * _GROUPS_PER_BODY + gg
                row = negs[pl.ds(g * nsb + ii, 1), :]
                sc = scores[gg]
                out.append(ranks[gg] + jnp.where(
                    row > sc, 1, jnp.where(row == sc, jnp.where(ii < jq, 1, 0), 0)))
            return tuple(out)

        ranks = lax.fori_loop(0, n_elig, rank_body,
                              tuple(jnp.zeros((nsb, tq), I32) for _ in range(_GROUPS_PER_BODY)))
        for gg in range(_GROUPS_PER_BODY):
            g = gp * _GROUPS_PER_BODY + gg
            neg = jnp.where(elig, jnp.where(ranks[gg] < N_SELECT, 0.0, 1.0), 1.0)
            negs[pl.ds(pl.multiple_of(g * nsb, nsb), nsb), :] = neg
        return carry

    lax.fori_loop(0, N_KV_GROUPS // _GROUPS_PER_BODY, cmp_pair, 0)

    for hd in range(N_HEADS):
        res[hd] = gate_ref[:, 3 * hd:3 * hd + 1] * acc[hd]

    neg_q = negs[...].T
    for hd in range(N_HEADS):
        g = hd // HEADS_PER_GROUP
        w = neg_q[:, (g // 2) * LANES:(g // 2 + 1) * LANES]
        if g % 2 == 0:
            w = pltpu.roll(w, HEAD_DIM, 1)
        cur_q = qaug[hd, :, 0:LANES].astype(F32)
        qaug[hd, :, 0:LANES] = jnp.where(lo_half, cur_q, w).astype(BF16)

    def reset_state():
        for hd in range(N_HEADS):
            mst[hd] = jnp.full((tq, LANES), NEG, F32)
            acc[hd] = jnp.zeros((tq, LANES), F32)

    def finalize(branch):
        for hd in range(N_HEADS):
            a = acc[hd]
            f = gate_ref[:, 3 * hd + branch:3 * hd + branch + 1] / a[:, HEAD_DIM:HEAD_DIM + 1]
            res[hd] = res[hd] + a * f

    def tile_aug(k0, width, onehot):
        kr = lax.broadcasted_iota(I32, (1, width), 1)
        dk = lax.shift_right_logical(k0, 8) - lax.shift_right_logical(q0, 8)
        bias = _bias_rows(jnp.bitwise_and(kr, 255).astype(F32),
                          (dk + lax.shift_right_logical(kr, 8)).astype(F32), width)
        if onehot:
            jrow = lax.broadcasted_iota(I32, (HEAD_DIM, width), 0)
            kcol = lax.broadcasted_iota(I32, (HEAD_DIM, width), 1)
            mid = jnp.where(jrow == lax.shift_right_logical(k0 + kcol, 6), NEG, 0.0).astype(BF16)
        else:
            mid = jnp.zeros((HEAD_DIM, width), BF16)
        return jnp.concatenate([mid, bias], axis=0)

    def run_tile(kt, slot_k, onehot, masked, gpb=_GROUPS_PER_BODY, width=tk):
        k0 = pl.multiple_of(kt * width, width)
        aug = tile_aug(k0, width, onehot)

        def pair_body(gp, carry):
            for gg in range(gpb):
                g = gp * gpb + gg
                r0 = pl.multiple_of(g * HEAD_DIM, HEAD_DIM)
                k_t = kvt_ref[pl.ds(slot_k * D_KV + r0, HEAD_DIM), pl.ds(k0, width)]
                v_t = kvt_ref[pl.ds((slot_k + 1) * D_KV + r0, HEAD_DIM), pl.ds(k0, width)]
                k_aug = jnp.concatenate([k_t, aug], axis=0)
                v_aug = jnp.concatenate([v_t, ones_blk[:, 0:width]], axis=0)
                for h in range(HEADS_PER_GROUP):
                    hd = g * HEADS_PER_GROUP + h
                    s = _dot(qaug[hd], k_aug)
                    if masked is not None:
                        s = jnp.where(masked, NEG, s)
                    _online_update(hd, s, v_aug, acc, mst, width // LANES)
            return carry

        lax.fori_loop(0, N_KV_GROUPS // gpb, pair_body, 0)

    reset_state()
    def tile_if(cond, kt, slot_k, onehot, masked):
        trips = jnp.where(cond, 1, 0)
        lax.fori_loop(0, trips, lambda _, c: (run_tile(kt, slot_k, onehot, masked, gpb=N_KV_GROUPS), c)[1], 0)

    lax.fori_loop(0, i, lambda kt, c: (run_tile(kt, 0, True, None, gpb=N_KV_GROUPS), c)[1], 0)
    tile_if(i >= 0, i, 0, True, above)
    finalize(1)

    reset_state()
    nback = WINDOW // tk
    tile_if(i >= nback, jnp.maximum(i - nback, 0), 2, False, below)
    for back in range(nback - 1, 0, -1):
        tile_if(i >= back, jnp.maximum(i - back, 0), 2, False, None)
    tile_if(i >= 0, i, 2, False, above)
    finalize(2)

    for w in range(N_HEADS // 2):
        ow = jnp.where(lo_half, res[2 * w], pltpu.roll(res[2 * w + 1], HEAD_DIM, 1))
        osc[:, w * LANES:(w + 1) * LANES] = (ow * zg_ref[:, w * LANES:(w + 1) * LANES]).astype(BF16)
    xn = x_ref[...] + _dot(osc[...], wout_ref[...])
    if final:
        xn = _rms(xn, fg_ref[...])
    o_ref[...] = xn


def nsa_prompt_attn(q, kvtb, cmpt, gates, zg, x, wout_bf, fg, *, final, tq=256):
    b, t, _ = q.shape
    ncmp = cmpt.shape[3]
    blk = lambda bi, i: (bi, i, 0)
    return pl.pallas_call(
        functools.partial(_nsa_prompt_body, tq=tq, final=final),
        grid=(b, t // tq),
        in_specs=[pl.BlockSpec((None, tq, D_MODEL), blk),
                  pl.BlockSpec((None, 4 * D_KV, t), lambda bi, i: (bi, 0, 0)),
                  pl.BlockSpec((None, 2, D_KV, ncmp), lambda bi, i: (bi, 0, 0, 0)),
                  pl.BlockSpec((None, tq, LANES), blk),
                  pl.BlockSpec((None, tq, D_MODEL), blk),
                  pl.BlockSpec((None, tq, D_MODEL), blk),
                  pl.BlockSpec((D_MODEL, D_MODEL), lambda bi, i: (0, 0)),
                  pl.BlockSpec((1, D_MODEL), lambda bi, i: (0, 0))],
        out_specs=pl.BlockSpec((None, tq, D_MODEL), blk),
        out_shape=jax.ShapeDtypeStruct((b, t, D_MODEL), F32),
        scratch_shapes=[pltpu.VMEM((N_HEADS, tq, 2 * LANES), BF16),
                        pltpu.VMEM((N_HEADS, tq, LANES), F32),
                        pltpu.VMEM((N_HEADS, tq, LANES), F32),
                        pltpu.VMEM((N_HEADS, tq, LANES), F32),
                        pltpu.VMEM((tq, D_MODEL), BF16),
                        pltpu.VMEM((N_KV_GROUPS * 64, tq), F32),
                        pltpu.VMEM((HEAD_DIM, tq), BF16)],
        compiler_params=_params("arbitrary", "arbitrary"),
        name="nsa_prompt_attn",
    )(q, kvtb, cmpt, gates, zg, x, wout_bf, fg)


def _nsa_sample_body(pt_ref, q_ref, kvn_ref, gate_ref, zg_ref, win_ref, w1q_ref, w2h_ref, c_ref,
                     cache_ref, o_ref, pbuf, sem, cmplo, cmphi, ebig, bias_c, bias_s, bias_w,
                     *, layer, npages):
    b = pl.program_id(0)
    nb = pl.num_programs(0)
    past = npages * PAGE_SIZE
    nchunk = past // CMP_STRIDE
    nsel = past + LANES
    nwin = WINDOW + LANES
    rows = N_HEADS * SPAD

    def page_copy(seq, p, slot):
        return pltpu.make_async_copy(cache_ref.at[layer, pt_ref[seq, p]],
                                     pbuf.at[slot, :, :, pl.ds(p * PAGE_SIZE, PAGE_SIZE)],
                                     sem.at[slot])

    def start_seq(seq, slot):
        for p in range(npages):
            page_copy(seq, p, slot).start()

    slot = lax.rem(b, 2)

    @pl.when(b == 0)
    def _():
        start_seq(0, 0)
        r = lax.broadcasted_iota(I32, (rows, 1), 0)
        slope = jnp.exp2(-0.5 * (lax.shift_right_logical(r, 3) + 1).astype(F32)) * LOG2E
        qpos = past + jnp.bitwise_and(r, SPAD - 1)
        ccol = lax.broadcasted_iota(I32, (rows, nchunk), 1)
        c_end = CMP_STRIDE * ccol + CMP_BLOCK - 1
        bias_c[...] = jnp.where(ccol < nchunk - 1, -slope * (qpos - c_end).astype(F32), NEG)
        kpos = lax.broadcasted_iota(I32, (rows, nsel), 1)
        bias_s[...] = jnp.where(kpos <= qpos, -slope * (qpos - kpos).astype(F32), NEG)
        wcol = lax.broadcasted_iota(I32, (rows, nwin), 1)
        wpos = jnp.where(wcol < WINDOW, past - WINDOW + wcol, past + wcol - WINDOW)
        dist_w = qpos - wpos
        ok = (dist_w >= 0) & (dist_w <= WINDOW) & (wcol < WINDOW + SPAD)
        bias_w[...] = jnp.where(ok, -slope * dist_w.astype(F32), NEG)
        ej = lax.broadcasted_iota(I32, (64, nsel), 0)
        ek = lax.shift_right_logical(lax.broadcasted_iota(I32, (64, nsel), 1), 6)
        ebig[...] = jnp.where(ej == ek, NEG, 0.0).astype(BF16)

    @pl.when(b + 1 < nb)
    def _():
        start_seq(b + 1, 1 - slot)

    for p in range(npages):
        page_copy(b, p, slot).wait()

    cmp_out = []
    for w in range(4):
        s, hf = divmod(w, 2)
        tok = pbuf[slot, s, hf * LANES:(hf + 1) * LANES, :].T
        tok3 = tok.reshape(nchunk, CMP_STRIDE, LANES)
        cmplo[w] = tok3[:, 0:8, :].reshape(nchunk * 8, LANES)
        cmphi[w] = tok3[:, 8:16, :].reshape(nchunk * 8, LANES)

        def load_x(l, w=w):
            src = cmplo if l < 8 else cmphi
            return src[w, pl.ds(l % 8, nchunk, stride=8), :]

        hids = _compress_hidden(load_x, w1q_ref.at[s], c_ref[s, 0:1, :], nchunk)
        out = _dot(hids[0], w2h_ref[s, 0]) + _dot(hids[1], w2h_ref[s, 1])
        rix = lax.broadcasted_iota(I32, (nchunk, LANES), 0)
        cmp_out.append(jnp.where(rix < nchunk - 1, out, 0.0))
    kc = jnp.concatenate(cmp_out[0:2], axis=1).astype(BF16)
    vc = jnp.concatenate(cmp_out[2:4], axis=1).astype(BF16)

    qf = q_ref[...].astype(F32)
    lane4 = lax.broadcasted_iota(I32, (SPAD, D_KV), 1)
    pieces = []
    for hd in range(N_HEADS):
        g, h = divmod(hd, HEADS_PER_GROUP)
        w = qf[:, g * D_KV:(g + 1) * D_KV]
        sh = ((g - h) * HEAD_DIM) % D_KV
        if sh:
            w = pltpu.roll(w, sh, 1)
        inside = (lane4 >= g * HEAD_DIM) & (lane4 < (g + 1) * HEAD_DIM)
        pieces.append(jnp.where(inside, w, 0.0))
    qbd = jnp.concatenate(pieces, axis=0).astype(BF16)

    def new_rows(c):
        new = kvn_ref[:, c * D_KV:(c + 1) * D_KV].astype(F32)
        return jnp.concatenate([new, jnp.zeros((LANES - SPAD, D_KV), F32)], axis=0).astype(BF16)

    s_c = _dot_nt(qbd, kc) + bias_c[...]
    m_c = jnp.max(s_c, axis=-1, keepdims=True)
    e_c = jnp.exp2(s_c - m_c)
    p_c = e_c / jnp.maximum(jnp.sum(e_c, axis=-1, keepdims=True), 1e-30)
    o_c = _dot(p_c.astype(BF16), vc)

    psum = jnp.concatenate(
        [sum(p_c[(g * HEADS_PER_GROUP + h) * SPAD:(g * HEADS_PER_GROUP + h + 1) * SPAD]
             for h in range(HEADS_PER_GROUP)) for g in range(N_KV_GROUPS)], axis=0)
    cj = lax.broadcasted_iota(I32, (nchunk, 64), 0)
    jj = lax.broadcasted_iota(I32, (nchunk, 64), 1)
    ov = jnp.where(CMP_STRIDE * cj < SEL_BLOCK * (jj + 1),
                   jnp.where(CMP_STRIDE * cj + CMP_BLOCK - 1 >= SEL_BLOCK * jj, 1.0, 0.0),
                   0.0).astype(BF16)
    p_hi = psum.astype(BF16)
    p_lo = (psum - p_hi.astype(F32)).astype(BF16)
    imp = _dot(p_hi, ov) + _dot(p_lo, ov)
    nrow = N_KV_GROUPS * SPAD
    jb = lax.broadcasted_iota(I32, (nrow, 64), 1)
    cur = lax.shift_right_logical(
        past + jnp.bitwise_and(lax.broadcasted_iota(I32, (nrow, 64), 0), SPAD - 1), 6)
    elig = jb <= cur
    forced = (jb == 0) | (jb == cur) | (jb == cur - 1)
    sc = jnp.where(elig, jnp.where(forced, FORCE_SCORE, imp), NEG)
    rank = jnp.zeros((nrow, 64), I32)
    for ii in range(past // SEL_BLOCK + 1):
        col = sc[:, ii:ii + 1]
        rank = rank + jnp.where(col > sc, 1, jnp.where(col == sc, jnp.where(ii < jb, 1, 0), 0))
    neg = jnp.where(elig, jnp.where(rank < N_SELECT, 0.0, 1.0), 1.0)
    neg_rows = jnp.concatenate(
        [neg[(hd // HEADS_PER_GROUP) * SPAD:(hd // HEADS_PER_GROUP + 1) * SPAD] for hd in range(N_HEADS)],
        axis=0).astype(BF16)

    k_new, v_new = new_rows(0), new_rows(1)
    s_s = (jnp.concatenate([_dot(qbd, pbuf[slot, 2].astype(BF16)), _dot_nt(qbd, k_new)], axis=1)
           + _dot(neg_rows, ebig[...]) + bias_s[...])
    m_s = jnp.max(s_s, axis=-1, keepdims=True)
    e_s = jnp.exp2(s_s - m_s)
    e_sb = e_s.astype(BF16)
    o_s = (_dot_nt(e_sb[:, 0:past], pbuf[slot, 3].astype(BF16)) + _dot(e_sb[:, past:], v_new)
           ) / jnp.sum(e_s, axis=-1, keepdims=True)

    kw_new, vw_new = new_rows(2), new_rows(3)
    s_w = jnp.concatenate([_dot(qbd, win_ref[0].astype(BF16)), _dot_nt(qbd, kw_new)], axis=1) + bias_w[...]
    m_w = jnp.max(s_w, axis=-1, keepdims=True)
    e_w = jnp.exp2(s_w - m_w)
    e_wb = e_w.astype(BF16)
    o_w = (_dot_nt(e_wb[:, 0:WINDOW], win_ref[1].astype(BF16)) + _dot(e_wb[:, WINDOW:], vw_new)
           ) / jnp.sum(e_w, axis=-1, keepdims=True)

    gates = gate_ref[...]
    gcol = lambda br: jnp.concatenate(
        [gates[:, 3 * hd + br:3 * hd + br + 1] for hd in range(N_HEADS)], axis=0)
    o_all = gcol(0) * o_c + gcol(1) * o_s + gcol(2) * o_w

    outs = []
    for g in range(N_KV_GROUPS):
        inside = (lane4 >= g * HEAD_DIM) & (lane4 < (g + 1) * HEAD_DIM)
        tot = jnp.zeros((SPAD, D_KV), F32)
        for h in range(HEADS_PER_GROUP):
            hd = g * HEADS_PER_GROUP + h
            piece = jnp.where(inside, o_all[hd * SPAD:(hd + 1) * SPAD], 0.0)
            sh = ((h - g) * HEAD_DIM) % D_KV
            if sh:
                piece = pltpu.roll(piece, sh, 1)
            tot = tot + piece
        outs.append(tot)
    o_ref[...] = jnp.concatenate(outs, axis=1) * zg_ref[...]


def nsa_sample_attn(page_table, q, kvn, gates, zg, win_t, w1q, w2h, const, cache_t, *, layer):
    nb, npages = page_table.shape
    past = npages * PAGE_SIZE
    nchunk = past // CMP_STRIDE
    rows = N_HEADS * SPAD
    seq = lambda i, pt: (i, 0)
    grid_spec = pltpu.PrefetchScalarGridSpec(
        num_scalar_prefetch=1,
        grid=(nb,),
        in_specs=[pl.BlockSpec((SPAD, D_MODEL), seq),
                  pl.BlockSpec((SPAD, 4 * D_KV), seq),
                  pl.BlockSpec((SPAD, LANES), seq),
                  pl.BlockSpec((SPAD, D_MODEL), seq),
                  pl.BlockSpec((None, None, 2, D_KV, WINDOW), lambda i, pt: (layer, i, 0, 0, 0)),
                  pl.BlockSpec((2, 4, 256, 256), lambda i, pt: (0, 0, 0, 0)),
                  pl.BlockSpec((2, 2, CMP_HIDDEN, LANES), lambda i, pt: (0, 0, 0, 0)),
                  pl.BlockSpec((2, 8, CMP_HIDDEN), lambda i, pt: (0, 0, 0)),
                  pl.BlockSpec(memory_space=pl.ANY)],
        out_specs=pl.BlockSpec((SPAD, D_MODEL), seq),
        scratch_shapes=[pltpu.VMEM((2, 4, D_KV, past), F32),
                        pltpu.SemaphoreType.DMA((2,)),
                        pltpu.VMEM((4, nchunk * 8, LANES), F32),
                        pltpu.VMEM((4, nchunk * 8, LANES), F32),
                        pltpu.VMEM((64, past + LANES), BF16),
                        pltpu.VMEM((rows, nchunk), F32),
                        pltpu.VMEM((rows, past + LANES), F32),
                        pltpu.VMEM((rows, WINDOW + LANES), F32)])
    return pl.pallas_call(
        functools.partial(_nsa_sample_body, layer=layer, npages=npages),
        grid_spec=grid_spec,
        out_shape=jax.ShapeDtypeStruct((nb * SPAD, D_MODEL), F32),
        compiler_params=_params("arbitrary"),
        name="nsa_sample_attn",
    )(page_table, q, kvn, gates, zg, win_t, w1q, w2h, const, cache_t)


def _outproj_body(o_ref, x_ref, w_ref, fg_ref, y_ref, *, final):
    xn = x_ref[...] + _dot(o_ref[...].astype(BF16), w_ref[...])
    if final:
        xn = _rms(xn, fg_ref[...])
    y_ref[...] = xn


def outproj(o, x, w_bf, fg, *, final, tm=256):
    n = o.shape[0]
    row = lambda i: (i, 0)
    return pl.pallas_call(
        functools.partial(_outproj_body, final=final),
        grid=(n // tm,),
        in_specs=[pl.BlockSpec((tm, D_MODEL), row), pl.BlockSpec((tm, D_MODEL), row),
                  pl.BlockSpec((D_MODEL, D_MODEL), lambda i: (0, 0)),
                  pl.BlockSpec((1, D_MODEL), lambda i: (0, 0))],
        out_specs=pl.BlockSpec((tm, D_MODEL), row),
        out_shape=jax.ShapeDtypeStruct((n, D_MODEL), F32),
        compiler_params=_params("arbitrary"),
        name="outproj",
    )(o, x, w_bf, fg)


def _prep_cmp_weights(pe, w1, w2):
    top = w1[:, :CMP_STRIDE].reshape(2, 4, 4 * HEAD_DIM, CMP_HIDDEN)
    bot = w1[:, CMP_STRIDE:].reshape(2, 4, 4 * HEAD_DIM, CMP_HIDDEN)
    w1q = jnp.concatenate([top, bot], axis=-1).astype(BF16)
    w2h = jnp.zeros((2, 2, CMP_HIDDEN, LANES), F32)
    for hf in range(2):
        w2h = w2h.at[:, hf, :, hf * HEAD_DIM:(hf + 1) * HEAD_DIM].set(w2)
    w2t = jnp.swapaxes(w2h, 2, 3)
    pe_pad = jnp.zeros((2, 8, CMP_BLOCK * HEAD_DIM), F32).at[:, 0].set(pe.reshape(2, -1))
    w1_flat = w1.reshape(2, CMP_BLOCK * HEAD_DIM, CMP_HIDDEN)
    return w1q, w2h.astype(BF16), w2t.astype(BF16), pe_pad.astype(BF16), w1_flat.astype(BF16)


def _prep_nsa_in_weights(w_in):
    gl = jnp.pad(w_in[:, 3584:], ((0, 0), (0, LANES - N_BRANCH * N_HEADS)))
    w_a = jnp.concatenate([w_in[:, 0:2048], w_in[:, 2048:2560], gl], axis=1).astype(BF16)
    w_t = w_in[:, 2048:3584].T.astype(BF16)
    w_n = w_in[:, 2560:3584].astype(BF16)
    return w_a, w_t, w_n


def kernel(x_prompt, x_sample, cache_kv, state_kv_win, state_conv, page_table, norm_g, final_norm_g,
           conv_w_in, conv_dw_w, conv_dw_b, conv_ln_g, conv_ln_b, conv_w_out,
           nsa_w_in, nsa_w_out, nsa_cmp_pe, nsa_cmp_w1, nsa_cmp_w2):
    bsz, seq, _ = x_prompt.shape
    nb = x_sample.shape[0]
    n_nsa = nsa_w_in.shape[0]
    n_pool = cache_kv.shape[1]
    win_keep = state_kv_win.shape[2]
    ns = SPAD * nb
    xp = x_prompt
    xs = jnp.pad(jnp.swapaxes(x_sample, 0, 1), ((0, SPAD - DEC_SEQ), (0, 0), (0, 0)))
    cache_t = jnp.transpose(cache_kv, (0, 1, 3, 4, 5, 2)).reshape(n_nsa, n_pool, 4, D_KV, PAGE_SIZE)
    win_state_t = jnp.transpose(state_kv_win, (0, 1, 3, 4, 5, 2)).reshape(n_nsa, nb, 2, D_KV, win_keep)
    conv_state_t = jnp.swapaxes(state_conv, 1, 2)
    fg = final_norm_g.reshape(1, D_MODEL)

    kv_p, kv_s, win_p, win_s, conv_p, conv_s = [], [], [], [], [], []
    for layer in range(DEPTH):
        j = layer // 2
        g = norm_g[layer].reshape(1, D_MODEL)
        last = layer == DEPTH - 1
        if layer % 2 == 0:
            w_in = conv_w_in[j].astype(BF16)
            w_out = conv_w_out[j].astype(BF16)
            db = conv_dw_b[j].reshape(1, D_CONV)
            lg = conv_ln_g[j].reshape(1, D_CONV)
            lb = conv_ln_b[j].reshape(1, D_CONV)
            v, zg = conv_inproj(xp.reshape(bsz * seq, D_MODEL), g, w_in)
            v = v.reshape(bsz, seq, D_CONV)
            xp = conv_prompt(v, zg.reshape(bsz, seq, D_CONV), xp, conv_dw_w[j], db, lg, lb, w_out)
            conv_p.append(v[:, seq - CONV_STATE:])
            vs, zgs = conv_inproj(xs.reshape(ns, D_MODEL), g, w_in)
            vs = vs.reshape(SPAD, nb, D_CONV)
            xs = conv_sample(conv_state_t, vs, zgs.reshape(SPAD, nb, D_CONV), xs, conv_dw_w[j],
                             db, lg, lb, w_out, layer=j)
            conv_s.append(vs[:DEC_SEQ])
        else:
            w_a, w_t, w_n = _prep_nsa_in_weights(nsa_w_in[j])
            w_out = nsa_w_out[j].astype(BF16)
            w1q, w2h, w2t, pe_pad, w1_flat = _prep_cmp_weights(nsa_cmp_pe[j], nsa_cmp_w1[j], nsa_cmp_w2[j])
            const = cmp_const(pe_pad, w1_flat)
            q, zg, gates, kvpt, kvwt, kcn, kvtb = nsa_inproj(xp, g, w_a, w_t)
            cmpt = compress_prompt(kcn, w1q, w2t, const)
            xp = nsa_prompt_attn(q, kvtb, cmpt, gates, zg, xp, w_out, fg, final=last)
            kv_p.append(jnp.transpose(kvpt.reshape(bsz, 4, N_KV_GROUPS, HEAD_DIM, seq), (0, 4, 1, 2, 3)))
            wtail = kvwt.reshape(bsz, 2, N_KV_GROUPS, HEAD_DIM, seq)[..., seq - min(WINDOW, seq):]
            win_p.append(jnp.transpose(wtail, (0, 4, 1, 2, 3)))
            xq = jnp.swapaxes(xs, 0, 1).reshape(ns, D_MODEL)
            qs, zgs, gs, kvpts, kvwts, kvns = nsa_inproj(xq.reshape(1, ns, D_MODEL), g, w_a, w_t, w_n)
            og = nsa_sample_attn(page_table, qs.reshape(ns, D_MODEL), kvns.reshape(ns, 4 * D_KV),
                                 gs.reshape(ns, LANES), zgs.reshape(ns, D_MODEL),
                                 win_state_t, w1q, w2h, const, cache_t, layer=j)
            xq = outproj(og, xq, w_out, fg, final=last)
            xs = jnp.swapaxes(xq.reshape(nb, SPAD, D_MODEL), 0, 1)
            kvps5 = kvpts.reshape(4, N_KV_GROUPS, HEAD_DIM, nb, SPAD)[..., :DEC_SEQ]
            kv_s.append(jnp.transpose(kvps5, (3, 4, 0, 1, 2)))
            kvws5 = kvwts.reshape(2, N_KV_GROUPS, HEAD_DIM, nb, SPAD)[..., :DEC_SEQ]
            new_w = jnp.transpose(kvws5, (3, 4, 0, 1, 2))
            win_s.append(new_w)
    y_sample = jnp.swapaxes(xs[:DEC_SEQ], 0, 1)
    win_sample = jnp.concatenate([state_kv_win, jnp.stack(win_s)], axis=2)[:, :, -win_keep:]
    conv_sample_out = jnp.swapaxes(
        jnp.concatenate([conv_state_t[:, DEC_SEQ:], jnp.stack(conv_s)], axis=1), 1, 2)
    return (xp, y_sample, jnp.stack(kv_p), jnp.stack(kv_s), jnp.stack(win_p), win_sample,
            jnp.stack(conv_p), conv_sample_out)
```

```python
import functools
import math

import jax
import jax.numpy as jnp
import ml_dtypes
import numpy as np
from jax import lax
from jax.experimental import pallas as pl
from jax.experimental.pallas import tpu as pltpu

F32 = jnp.float32
BF16 = jnp.bfloat16
I32 = jnp.int32

D_MODEL = 1024
DEPTH = 4
DEC_SEQ = 4
PAGE_SIZE = 128
D_CONV = 2048
CONV_WIDTH = 31
CONV_STATE = CONV_WIDTH - 1
N_HEADS = 16
HEAD_DIM = 64
N_KV_GROUPS = 4
HEADS_PER_GROUP = 4
D_KV = N_KV_GROUPS * HEAD_DIM
CMP_BLOCK = 32
CMP_STRIDE = 16
CMP_HIDDEN = 128
SEL_BLOCK = 64
N_SELECT = 16
WINDOW = 512
N_BRANCH = 3
NORM_EPS = 1e-6
FORCE_SCORE = 1e4
NEG = -1e30
LOG2E = math.log2(math.e)
SPAD = 8
LANES = 128
VMEM_LIMIT = 56 * 1024 * 1024


def _params(*sem):
    return pltpu.CompilerParams(dimension_semantics=sem, vmem_limit_bytes=VMEM_LIMIT)


def _rms(x, g):
    return x * lax.rsqrt(jnp.mean(x * x, axis=-1, keepdims=True) + NORM_EPS) * g


def _silu(x):
    return x * jax.nn.sigmoid(x)


def _dot(a, b):
    return jnp.dot(a, b, preferred_element_type=F32)


def _dot_nt(a, b):
    return lax.dot_general(a, b, (((1,), (1,)), ((), ())), preferred_element_type=F32)


def _rep(x, n):
    return jnp.concatenate([x] * n, axis=1) if n > 1 else x


def _slope2(hd):
    return 2.0 ** (-(hd + 1) / 2.0) * LOG2E


def _hi_lo(x):
    hi = float(np.float32(x).astype(ml_dtypes.bfloat16).astype(np.float32))
    lo = float(np.float32(x - hi).astype(ml_dtypes.bfloat16).astype(np.float32))
    return hi, lo


def _conv_inproj_body(x_ref, g_ref, w_ref, v_ref, zg_ref):
    h = _rms(x_ref[...], g_ref[...]).astype(BF16)
    ch = 512
    for c in range(D_CONV // ch):
        lo = c * ch
        a = _dot(h, w_ref[:, lo:lo + ch])
        gl = _dot(h, w_ref[:, D_CONV + lo:D_CONV + lo + ch])
        z = _dot(h, w_ref[:, 2 * D_CONV + lo:2 * D_CONV + lo + ch])
        v_ref[:, lo:lo + ch] = a * jax.nn.sigmoid(gl)
        zg_ref[:, lo:lo + ch] = _silu(z)


def conv_inproj(x, g, w_bf, tm=256):
    n = x.shape[0]
    return pl.pallas_call(
        _conv_inproj_body,
        grid=(n // tm,),
        in_specs=[pl.BlockSpec((tm, D_MODEL), lambda i: (i, 0)),
                  pl.BlockSpec((1, D_MODEL), lambda i: (0, 0)),
                  pl.BlockSpec((D_MODEL, 3 * D_CONV), lambda i: (0, 0))],
        out_specs=[pl.BlockSpec((tm, D_CONV), lambda i: (i, 0)),
                   pl.BlockSpec((tm, D_CONV), lambda i: (i, 0))],
        out_shape=[jax.ShapeDtypeStruct((n, D_CONV), F32),
                   jax.ShapeDtypeStruct((n, D_CONV), F32)],
        compiler_params=_params("arbitrary"),
        name="conv_inproj",
    )(x, g, w_bf)


_HDR = 32
_RW = 128
_PH = 4
_NSLAB = D_CONV // LANES
_GROUPS_PER_BODY = 4


def _ln_gate(c, lg, lb, zg):
    mu = jnp.mean(c, axis=-1, keepdims=True)
    xc = c - mu
    var = jnp.mean(xc * xc, axis=-1, keepdims=True)
    y = xc * lax.rsqrt(var + NORM_EPS) * lg + lb
    return _silu(y) * zg


def _conv_prompt_body(v_ref, zg_ref, x_ref, dw_ref, db_ref, lg_ref, lb_ref, wout_ref, o_ref,
                      vbuf, cbuf, ybuf, *, tt):
    i = pl.program_id(1)

    @pl.when(i == 0)
    def _():
        vbuf[:, 0:_HDR, :] = jnp.zeros((_NSLAB, _HDR, LANES), F32)

    @pl.when(i > 0)
    def _():
        vbuf[:, 0:_HDR, :] = vbuf[:, tt:tt + _HDR, :]

    for w in range(_NSLAB):
        vbuf[w, _HDR:_HDR + tt, :] = v_ref[:, w * LANES:(w + 1) * LANES]
    off = _HDR - CONV_STATE

    def row_body(r, carry):
        r0 = pl.multiple_of(r * _RW, _RW)

        def slab_body(w, carry2):
            c0 = pl.multiple_of(w * LANES, LANES)
            starts = [r0 + sub * 8 * _PH + u for sub in range(_RW // (8 * _PH)) for u in range(_PH)]
            accs = [[None, None] for _ in starts]
            for j in range(CONV_WIDTH):
                wj = dw_ref[pl.ds(j, 1), pl.ds(c0, LANES)]
                for a, st in enumerate(starts):
                    term = vbuf[w, pl.ds(st + j + off, 8, stride=_PH), :] * wj
                    prev = accs[a][j % 2]
                    accs[a][j % 2] = term if prev is None else prev + term
            bias = db_ref[:, pl.ds(c0, LANES)]
            for a, st in enumerate(starts):
                cbuf[w, pl.ds(st, 8, stride=_PH), :] = (accs[a][0] + bias) + accs[a][1]
            return carry2

        lax.fori_loop(0, _NSLAB, slab_body, 0)
        c = jnp.concatenate([cbuf[w, pl.ds(r0, _RW), :] for w in range(_NSLAB)], axis=1)
        y = _ln_gate(c, lg_ref[...], lb_ref[...], zg_ref[pl.ds(r0, _RW), :])
        ybuf[pl.ds(r0, _RW), :] = y.astype(BF16)
        return carry

    lax.fori_loop(0, tt // _RW, row_body, 0)
    o_ref[...] = x_ref[...] + _dot(ybuf[...], wout_ref[...])


def conv_prompt(v, zg, x, dw, db, lg, lb, wout_bf, tt=512):
    b, t, _ = v.shape
    return pl.pallas_call(
        functools.partial(_conv_prompt_body, tt=tt),
        grid=(b, t // tt),
        in_specs=[pl.BlockSpec((None, tt, D_CONV), lambda bi, i: (bi, i, 0)),
                  pl.BlockSpec((None, tt, D_CONV), lambda bi, i: (bi, i, 0)),
                  pl.BlockSpec((None, tt, D_MODEL), lambda bi, i: (bi, i, 0)),
                  pl.BlockSpec((CONV_WIDTH, D_CONV), lambda bi, i: (0, 0)),
                  pl.BlockSpec((1, D_CONV), lambda bi, i: (0, 0)),
                  pl.BlockSpec((1, D_CONV), lambda bi, i: (0, 0)),
                  pl.BlockSpec((1, D_CONV), lambda bi, i: (0, 0)),
                  pl.BlockSpec((D_CONV, D_MODEL), lambda bi, i: (0, 0))],
        out_specs=pl.BlockSpec((None, tt, D_MODEL), lambda bi, i: (bi, i, 0)),
        out_shape=jax.ShapeDtypeStruct((b, t, D_MODEL), F32),
        scratch_shapes=[pltpu.VMEM((_NSLAB, tt + _HDR, LANES), F32),
                        pltpu.VMEM((_NSLAB, tt, LANES), F32),
                        pltpu.VMEM((tt, D_CONV), BF16)],
        compiler_params=_params("arbitrary", "arbitrary"),
        name="conv_prompt",
    )(v, zg, x, dw, db, lg, lb, wout_bf)


def _conv_sample_body(st_ref, v_ref, zg_ref, x_ref, dw_ref, db_ref, lg_ref, lb_ref, wout_ref,
                      o_ref, ybuf, *, bs):
    ch = 512
    for t in range(DEC_SEQ):
        parts = []
        for c in range(D_CONV // ch):
            cs = slice(c * ch, (c + 1) * ch)
            acc = jnp.zeros((bs, ch), F32)
            for j in range(CONV_WIDTH):
                r = t + j
                src = st_ref[r, :, cs] if r < CONV_STATE else v_ref[r - CONV_STATE, :, cs]
                acc = acc + src * dw_ref[j:j + 1, cs]
            parts.append(acc)
        c_t = jnp.concatenate(parts, axis=1) + db_ref[...]
        ybuf[t * bs:(t + 1) * bs, :] = _ln_gate(c_t, lg_ref[...], lb_ref[...], zg_ref[t])
    y = _dot(ybuf[...].astype(BF16), wout_ref[...])
    for t in range(DEC_SEQ):
        o_ref[t] = x_ref[t] + y[t * bs:(t + 1) * bs]
    for t in range(DEC_SEQ, SPAD):
        o_ref[t] = jnp.zeros((bs, D_MODEL), F32)


def conv_sample(state_t, v, zg, x, dw, db, lg, lb, wout_bf, *, layer, bs=16):
    nb = state_t.shape[2]
    blk = lambda i: (0, i, 0)
    return pl.pallas_call(
        functools.partial(_conv_sample_body, bs=bs),
        grid=(nb // bs,),
        in_specs=[pl.BlockSpec((None, CONV_STATE, bs, D_CONV), lambda i: (layer, 0, i, 0)),
                  pl.BlockSpec((DEC_SEQ, bs, D_CONV), blk),
                  pl.BlockSpec((DEC_SEQ, bs, D_CONV), blk),
                  pl.BlockSpec((SPAD, bs, D_MODEL), blk),
                  pl.BlockSpec((CONV_WIDTH, D_CONV), lambda i: (0, 0)),
                  pl.BlockSpec((1, D_CONV), lambda i: (0, 0)),
                  pl.BlockSpec((1, D_CONV), lambda i: (0, 0)),
                  pl.BlockSpec((1, D_CONV), lambda i: (0, 0)),
                  pl.BlockSpec((D_CONV, D_MODEL), lambda i: (0, 0))],
        out_specs=pl.BlockSpec((SPAD, bs, D_MODEL), blk),
        out_shape=jax.ShapeDtypeStruct((SPAD, nb, D_MODEL), F32),
        scratch_shapes=[pltpu.VMEM((DEC_SEQ * bs, D_CONV), F32)],
        compiler_params=_params("arbitrary"),
        name="conv_sample",
    )(state_t, v, zg, x, dw, db, lg, lb, wout_bf)


def _nsa_inproj_body(*refs, sample):
    if sample:
        x_ref, g_ref, wa_ref, wt_ref, wn_ref, q_ref, zg_ref, gate_ref, kvp_ref, kvw_ref, kvn_ref = refs
    else:
        x_ref, g_ref, wa_ref, wt_ref, q_ref, zg_ref, gate_ref, kvp_ref, kvw_ref, kcn_ref, kvtb_ref = refs
    h = _rms(x_ref[...], g_ref[...]).astype(BF16)
    ch = 512
    for c in range(2):
        q = _dot(h, wa_ref[:, c * ch:(c + 1) * ch])
        q_ref[:, c * ch:(c + 1) * ch] = (q * (HEAD_DIM ** -0.5 * LOG2E)).astype(BF16)
    for c in range(2):
        z = _dot(h, wa_ref[:, D_MODEL + c * ch:D_MODEL + (c + 1) * ch])
        zg_ref[:, c * ch:(c + 1) * ch] = _silu(z)
    gate_ref[...] = jax.nn.sigmoid(_dot(h, wa_ref[:, 2560:2688]))
    for c in range(3):
        kvt = _dot_nt(wt_ref[c * ch:(c + 1) * ch, :], h)
        if c < 2:
            kvp_ref[c * ch:(c + 1) * ch, :] = kvt
        else:
            kvw_ref[...] = kvt
        if not sample and c >= 1:
            kvtb_ref[(c - 1) * ch:c * ch, :] = kvt.astype(BF16)
    if sample:
        for c in range(2):
            kvn_ref[:, c * ch:(c + 1) * ch] = _dot(h, wn_ref[:, c * ch:(c + 1) * ch]).astype(BF16)
    else:
        kcn_ref[...] = _dot(h, wa_ref[:, 2048:2560])


def nsa_inproj(x, g, w_a, w_t, w_n=None, tm=256):
    b, t, _ = x.shape
    sample = w_n is not None
    tok = lambda bi, i: (bi, i, 0)
    const = lambda bi, i: (0, 0)
    in_specs = [pl.BlockSpec((None, tm, D_MODEL), tok),
                pl.BlockSpec((1, D_MODEL), const),
                pl.BlockSpec((D_MODEL, 2688), const),
                pl.BlockSpec((6 * D_KV, D_MODEL), const)]
    out_specs = [pl.BlockSpec((None, tm, D_MODEL), tok),
                 pl.BlockSpec((None, tm, D_MODEL), tok),
                 pl.BlockSpec((None, tm, LANES), tok),
                 pl.BlockSpec((None, 4 * D_KV, tm), lambda bi, i: (bi, 0, i)),
                 pl.BlockSpec((None, 2 * D_KV, tm), lambda bi, i: (bi, 0, i))]
    out_shape = [jax.ShapeDtypeStruct((b, t, D_MODEL), BF16),
                 jax.ShapeDtypeStruct((b, t, D_MODEL), F32),
                 jax.ShapeDtypeStruct((b, t, LANES), F32),
                 jax.ShapeDtypeStruct((b, 4 * D_KV, t), F32),
                 jax.ShapeDtypeStruct((b, 2 * D_KV, t), F32)]
    args = [x, g, w_a, w_t]
    if sample:
        in_specs.append(pl.BlockSpec((D_MODEL, 4 * D_KV), const))
        out_specs.append(pl.BlockSpec((None, tm, 4 * D_KV), tok))
        out_shape.append(jax.ShapeDtypeStruct((b, t, 4 * D_KV), BF16))
        args.append(w_n)
    else:
        out_specs.append(pl.BlockSpec((None, tm, 2 * D_KV), tok))
        out_shape.append(jax.ShapeDtypeStruct((b, t, 2 * D_KV), F32))
        out_specs.append(pl.BlockSpec((None, 4 * D_KV, tm), lambda bi, i: (bi, 0, i)))
        out_shape.append(jax.ShapeDtypeStruct((b, 4 * D_KV, t), BF16))
    return pl.pallas_call(
        functools.partial(_nsa_inproj_body, sample=sample),
        grid=(b, t // tm),
        in_specs=in_specs, out_specs=out_specs, out_shape=out_shape,
        compiler_params=_params("arbitrary", "arbitrary"),
        name="nsa_inproj_sample" if sample else "nsa_inproj",
    )(*args)


def _cmp_const_body(pe_ref, w1_ref, o_ref):
    o_ref[...] = _dot(pe_ref[...], w1_ref[...])


def cmp_const(pe_pad_bf, w1_flat_bf):
    return pl.pallas_call(
        _cmp_const_body,
        grid=(2,),
        in_specs=[pl.BlockSpec((None, 8, 2048), lambda s: (s, 0, 0)),
                  pl.BlockSpec((None, 2048, CMP_HIDDEN), lambda s: (s, 0, 0))],
        out_specs=pl.BlockSpec((None, 8, CMP_HIDDEN), lambda s: (s, 0, 0)),
        out_shape=jax.ShapeDtypeStruct((2, 8, CMP_HIDDEN), F32),
        compiler_params=_params("arbitrary"),
        name="cmp_const",
    )(pe_pad_bf, w1_flat_bf)


def _compress_hidden(load_x, w1q_ref, const_row, n):
    accs = [jnp.zeros((n, 2 * CMP_HIDDEN), F32) for _ in range(2)]
    for qd in range(4):
        xs = [load_x(4 * qd + j) for j in range(4)]
        w = w1q_ref[qd]
        for hf in range(2):
            lhs = jnp.concatenate([x[:, hf * HEAD_DIM:(hf + 1) * HEAD_DIM] for x in xs], axis=1)
            accs[hf] = accs[hf] + _dot(lhs.astype(BF16), w)
    hids = []
    for hf in range(2):
        first = accs[hf][:, :CMP_HIDDEN]
        second = pltpu.roll(accs[hf][:, CMP_HIDDEN:], n - 1, 0)
        hids.append(_silu(first + second + const_row).astype(BF16))
    return hids


def _compress_prompt_body(x_ref, w1q_ref, w2t_ref, c_ref, o_ref, *, n):
    load_x = lambda l: x_ref[pl.ds(l, n, stride=CMP_STRIDE), :]
    hids = _compress_hidden(load_x, w1q_ref, c_ref[0:1, :], n)
    out_t = _dot_nt(w2t_ref[0], hids[0]) + _dot_nt(w2t_ref[1], hids[1])
    cols = lax.broadcasted_iota(I32, (LANES, n), 1)
    o_ref[...] = jnp.where(cols < n - 1, out_t, 0.0).astype(BF16)


def compress_prompt(kcn, w1q, w2t, const):
    b, t, _ = kcn.shape
    n = t // CMP_STRIDE
    return pl.pallas_call(
        functools.partial(_compress_prompt_body, n=n),
        grid=(b, 2, 2),
        in_specs=[pl.BlockSpec((None, t, LANES), lambda bi, s, w: (bi, 0, 2 * s + w)),
                  pl.BlockSpec((None, 4, 256, 256), lambda bi, s, w: (s, 0, 0, 0)),
                  pl.BlockSpec((None, 2, LANES, CMP_HIDDEN), lambda bi, s, w: (s, 0, 0, 0)),
                  pl.BlockSpec((None, 8, CMP_HIDDEN), lambda bi, s, w: (s, 0, 0))],
        out_specs=pl.BlockSpec((None, None, LANES, n), lambda bi, s, w: (bi, s, w, 0)),
        out_shape=jax.ShapeDtypeStruct((b, 2, D_KV, n), BF16),
        compiler_params=_params("arbitrary", "arbitrary", "arbitrary"),
        name="compress_prompt",
    )(kcn, w1q, w2t, const)


def _bias_rows(pos_lo, pos_hi, nkeys):
    ri = lax.broadcasted_iota(I32, (LANES, nkeys), 0)
    return jnp.where(ri < 2, pos_lo, jnp.where(ri < 4, pos_hi, 0.0)).astype(BF16)


def _online_update(hd, s, v_aug, acc, mst, nrep):
    m_old = mst[hd]
    m_new = jnp.maximum(m_old, jnp.max(s, axis=-1, keepdims=True))
    p = jnp.exp2(s - _rep(m_new, nrep))
    alpha = jnp.exp2(m_old - m_new)
    acc[hd] = alpha * acc[hd] + _dot_nt(p.astype(BF16), v_aug)
    mst[hd] = m_new


def _nsa_prompt_body(q_ref, kvt_ref, cmp_ref, gate_ref, zg_ref, x_ref, wout_ref, fg_ref,
                     o_ref, qaug, acc, res, mst, osc, negs, ones_blk, *, tq, final):
    i = pl.program_id(1)
    q0 = i * tq
    tk = tq
    nrep = tk // LANES
    lane = lax.broadcasted_iota(I32, (tq, LANES), 1)
    lo_half = lane < HEAD_DIM
    kid = lax.broadcasted_iota(I32, (tq, tk), 1)
    qid = lax.broadcasted_iota(I32, (tq, tk), 0)
    above = kid > qid
    below = kid < qid

    orow = lax.broadcasted_iota(I32, (HEAD_DIM, tk), 0)
    ones_blk[...] = jnp.where(orow == 0, 1.0, 0.0).astype(BF16)
    zeros_blk = jnp.zeros((HEAD_DIM, tk), BF16)

    for hd in range(N_HEADS):
        w = q_ref[:, (hd // 2) * LANES:(hd // 2 + 1) * LANES].astype(F32)
        if hd % 2 == 1:
            w = pltpu.roll(w, HEAD_DIM, 1)
        qaug[hd, :, 0:LANES] = jnp.where(lo_half, w, 0.0).astype(BF16)
        hi, lo = _hi_lo(_slope2(hd))
        cols = jnp.where(lane == 0, hi, jnp.where(lane == 1, lo, jnp.where(
            lane == 2, 256.0 * hi, jnp.where(lane == 3, 256.0 * lo, 0.0))))
        qaug[hd, :, LANES:2 * LANES] = cols.astype(BF16)

    ncmp = cmp_ref.shape[2]
    cid = lax.broadcasted_iota(I32, (tq, ncmp), 1)
    c_end = CMP_STRIDE * cid + CMP_BLOCK - 1
    mask_c = c_end <= q0 + lax.broadcasted_iota(I32, (tq, ncmp), 0)
    ce1 = CMP_STRIDE * lax.broadcasted_iota(I32, (1, ncmp), 1) + CMP_BLOCK - 1
    aug_c = jnp.concatenate(
        [zeros_blk[:, 0:ncmp],
         _bias_rows(jnp.bitwise_and(ce1, 255).astype(F32),
                    (lax.shift_right_logical(ce1, 8) - lax.shift_right_logical(q0, 8)).astype(F32),
                    ncmp)], axis=0)
    nsb = 64
    jt = lax.broadcasted_iota(I32, (nsb, ncmp), 0)
    ct = lax.broadcasted_iota(I32, (nsb, ncmp), 1)
    ov_t = jnp.where(CMP_STRIDE * ct < SEL_BLOCK * (jt + 1),
                     jnp.where(CMP_STRIDE * ct + CMP_BLOCK - 1 >= SEL_BLOCK * jt, 1.0, 0.0),
                     0.0).astype(BF16)
    jq = lax.broadcasted_iota(I32, (nsb, tq), 0)
    cur = lax.shift_right_logical(q0 + lax.broadcasted_iota(I32, (nsb, tq), 1), 6)
    elig = jq <= cur
    forced = (jq == 0) | (jq == cur) | (jq == cur - 1)
    n_elig = (i + 1) * (tq // SEL_BLOCK)

    def cmp_pair(gp, carry):
        scores = []
        for gg in range(_GROUPS_PER_BODY):
            g = gp * _GROUPS_PER_BODY + gg
            r0 = pl.multiple_of(g * HEAD_DIM, HEAD_DIM)
            kc_aug = jnp.concatenate([cmp_ref[0, pl.ds(r0, HEAD_DIM), :], aug_c], axis=0)
            vc_aug = jnp.concatenate([cmp_ref[1, pl.ds(r0, HEAD_DIM), :], zeros_blk[:, 0:ncmp]], axis=0)
            psum = jnp.zeros((tq, ncmp), F32)
            for h in range(HEADS_PER_GROUP):
                hd = g * HEADS_PER_GROUP + h
                s = jnp.where(mask_c, _dot(qaug[hd], kc_aug), NEG)
                m = jnp.max(s, axis=-1, keepdims=True)
                e = jnp.where(mask_c, jnp.exp2(s - m), 0.0)
                p = e / jnp.maximum(jnp.sum(e, axis=-1, keepdims=True), 1e-30)
                psum = psum + p
                acc[hd] = _dot_nt(p.astype(BF16), vc_aug)
            p_hi = psum.astype(BF16)
            p_lo = (psum - p_hi.astype(F32)).astype(BF16)
            imp_t = _dot_nt(ov_t, p_hi) + _dot_nt(ov_t, p_lo)
            sc = jnp.where(elig, jnp.where(forced, FORCE_SCORE, imp_t), NEG)
            negs[pl.ds(pl.multiple_of(g * nsb, nsb), nsb), :] = sc
            scores.append(sc)

        def rank_body(ii, ranks):
            out = []
            for gg in range(_GROUPS_PER_BODY):
                g = gp * _GROUPS_PER_BODY + gg
                row = negs[pl.ds(g * nsb + ii, 1), :]
                sc = scores[gg]
                out.append(ranks[gg] + jnp.where(
                    row > sc, 1, jnp.where(row == sc, jnp.where(ii < jq, 1, 0), 0)))
            return tuple(out)

        ranks = lax.fori_loop(0, n_elig, rank_body,
                              tuple(jnp.zeros((nsb, tq), I32) for _ in range(_GROUPS_PER_BODY)))
        for gg in range(_GROUPS_PER_BODY):
            g = gp * _GROUPS_PER_BODY + gg
            neg = jnp.where(elig, jnp.where(ranks[gg] < N_SELECT, 0.0, 1.0), 1.0)
            negs[pl.ds(pl.multiple_of(g * nsb, nsb), nsb), :] = neg
        return carry

    lax.fori_loop(0, N_KV_GROUPS // _GROUPS_PER_BODY, cmp_pair, 0)

    for hd in range(N_HEADS):
        res[hd] = gate_ref[:, 3 * hd:3 * hd + 1] * acc[hd]

    neg_q = negs[...].T
    for hd in range(N_HEADS):
        g = hd // HEADS_PER_GROUP
        w = neg_q[:, (g // 2) * LANES:(g // 2 + 1) * LANES]
        if g % 2 == 0:
            w = pltpu.roll(w, HEAD_DIM, 1)
        cur_q = qaug[hd, :, 0:LANES].astype(F32)
        qaug[hd, :, 0:LANES] = jnp.where(lo_half, cur_q, w).astype(BF16)

    def reset_state():
        for hd in range(N_HEADS):
            mst[hd] = jnp.full((tq, LANES), NEG, F32)
            acc[hd] = jnp.zeros((tq, LANES), F32)

    def finalize(branch):
        for hd in range(N_HEADS):
            a = acc[hd]
            f = gate_ref[:, 3 * hd + branch:3 * hd + branch + 1] / a[:, HEAD_DIM:HEAD_DIM + 1]
            res[hd] = res[hd] + a * f

    def tile_aug(k0, width, onehot):
        kr = lax.broadcasted_iota(I32, (1, width), 1)
        dk = lax.shift_right_logical(k0, 8) - lax.shift_right_logical(q0, 8)
        bias = _bias_rows(jnp.bitwise_and(kr, 255).astype(F32),
                          (dk + lax.shift_right_logical(kr, 8)).astype(F32), width)
        if onehot:
            jrow = lax.broadcasted_iota(I32, (HEAD_DIM, width), 0)
            kcol = lax.broadcasted_iota(I32, (HEAD_DIM, width), 1)
            mid = jnp.where(jrow == lax.shift_right_logical(k0 + kcol, 6), NEG, 0.0).astype(BF16)
        else:
            mid = jnp.zeros((HEAD_DIM, width), BF16)
        return jnp.concatenate([mid, bias], axis=0)

    def run_tile(kt, slot_k, onehot, masked, gpb=_GROUPS_PER_BODY, width=tk):
        k0 = pl.multiple_of(kt * width, width)
        aug = tile_aug(k0, width, onehot)

        def pair_body(gp, carry):
            for gg in range(gpb):
                g = gp * gpb + gg
                r0 = pl.multiple_of(g * HEAD_DIM, HEAD_DIM)
                k_t = kvt_ref[pl.ds(slot_k * D_KV + r0, HEAD_DIM), pl.ds(k0, width)]
                v_t = kvt_ref[pl.ds((slot_k + 1) * D_KV + r0, HEAD_DIM), pl.ds(k0, width)]
                k_aug = jnp.concatenate([k_t, aug], axis=0)
                v_aug = jnp.concatenate([v_t, ones_blk[:, 0:width]], axis=0)
                for h in range(HEADS_PER_GROUP):
                    hd = g * HEADS_PER_GROUP + h
                    s = _dot(qaug[hd], k_aug)
                    if masked is not None:
                        s = jnp.where(masked, NEG, s)
                    _online_update(hd, s, v_aug, acc, mst, width // LANES)
            return carry

        lax.fori_loop(0, N_KV_GROUPS // gpb, pair_body, 0)

    reset_state()
    def tile_if(cond, kt, slot_k, onehot, masked):
        trips = jnp.where(cond, 1, 0)
        lax.fori_loop(0, trips, lambda _, c: (run_tile(kt, slot_k, onehot, masked, gpb=N_KV_GROUPS), c)[1], 0)

    lax.fori_loop(0, i, lambda kt, c: (run_tile(kt, 0, True, None, gpb=N_KV_GROUPS), c)[1], 0)
    tile_if(i >= 0, i, 0, True, above)
    finalize(1)

    reset_state()
    nback = WINDOW // tk
    tile_if(i >= nback, jnp.maximum(i - nback, 0), 2, False, below)
    for back in range(nback - 1, 0, -1):
        tile_if(i >= back, jnp.maximum(i - back, 0), 2, False, None)
    tile_if(i >= 0, i, 2, False, above)
    finalize(2)

    for w in range(N_HEADS // 2):
        ow = jnp.where(lo_half, res[2 * w], pltpu.roll(res[2 * w + 1], HEAD_DIM, 1))
        osc[:, w * LANES:(w + 1) * LANES] = (ow * zg_ref[:, w * LANES:(w + 1) * LANES]).astype(BF16)
    xn = x_ref[...] + _dot(osc[...], wout_ref[...])
    if final:
        xn = _rms(xn, fg_ref[...])
    o_ref[...] = xn


def nsa_prompt_attn(q, kvtb, cmpt, gates, zg, x, wout_bf, fg, *, final, tq=256):
    b, t, _ = q.shape
    ncmp = cmpt.shape[3]
    blk = lambda bi, i: (bi, i, 0)
    return pl.pallas_call(
        functools.partial(_nsa_prompt_body, tq=tq, final=final),
        grid=(b, t // tq),
        in_specs=[pl.BlockSpec((None, tq, D_MODEL), blk),
                  pl.BlockSpec((None, 4 * D_KV, t), lambda bi, i: (bi, 0, 0)),
                  pl.BlockSpec((None, 2, D_KV, ncmp), lambda bi, i: (bi, 0, 0, 0)),
                  pl.BlockSpec((None, tq, LANES), blk),
                  pl.BlockSpec((None, tq, D_MODEL), blk),
                  pl.BlockSpec((None, tq, D_MODEL), blk),
                  pl.BlockSpec((D_MODEL, D_MODEL), lambda bi, i: (0, 0)),
                  pl.BlockSpec((1, D_MODEL), lambda bi, i: (0, 0))],
        out_specs=pl.BlockSpec((None, tq, D_MODEL), blk),
        out_shape=jax.ShapeDtypeStruct((b, t, D_MODEL), F32),
        scratch_shapes=[pltpu.VMEM((N_HEADS, tq, 2 * LANES), BF16),
                        pltpu.VMEM((N_HEADS, tq, LANES), F32),
                        pltpu.VMEM((N_HEADS, tq, LANES), F32),
                        pltpu.VMEM((N_HEADS, tq, LANES), F32),
                        pltpu.VMEM((tq, D_MODEL), BF16),
                        pltpu.VMEM((N_KV_GROUPS * 64, tq), F32),
                        pltpu.VMEM((HEAD_DIM, tq), BF16)],
        compiler_params=_params("arbitrary", "arbitrary"),
        name="nsa_prompt_attn",
    )(q, kvtb, cmpt, gates, zg, x, wout_bf, fg)


def _nsa_sample_body(pt_ref, q_ref, kvn_ref, gate_ref, zg_ref, win_ref, w1q_ref, w2h_ref, c_ref,
                     cache_ref, o_ref, pbuf, sem, cmplo, cmphi, ebig, bias_c, bias_s, bias_w,
                     *, layer, npages):
    b = pl.program_id(0)
    nb = pl.num_programs(0)
    past = npages * PAGE_SIZE
    nchunk = past // CMP_STRIDE
    nsel = past + LANES
    nwin = WINDOW + LANES
    rows = N_HEADS * SPAD

    def page_copy(seq, p, slot):
        return pltpu.make_async_copy(cache_ref.at[layer, pt_ref[seq, p]],
                                     pbuf.at[slot, :, :, pl.ds(p * PAGE_SIZE, PAGE_SIZE)],
                                     sem.at[slot])

    def start_seq(seq, slot):
        for p in range(npages):
            page_copy(seq, p, slot).start()

    slot = lax.rem(b, 2)

    @pl.when(b == 0)
    def _():
        start_seq(0, 0)
        r = lax.broadcasted_iota(I32, (rows, 1), 0)
        slope = jnp.exp2(-0.5 * (lax.shift_right_logical(r, 3) + 1).astype(F32)) * LOG2E
        qpos = past + jnp.bitwise_and(r, SPAD - 1)
        ccol = lax.broadcasted_iota(I32, (rows, nchunk), 1)
        c_end = CMP_STRIDE * ccol + CMP_BLOCK - 1
        bias_c[...] = jnp.where(ccol < nchunk - 1, -slope * (qpos - c_end).astype(F32), NEG)
        kpos = lax.broadcasted_iota(I32, (rows, nsel), 1)
        bias_s[...] = jnp.where(kpos <= qpos, -slope * (qpos - kpos).astype(F32), NEG)
        wcol = lax.broadcasted_iota(I32, (rows, nwin), 1)
        wpos = jnp.where(wcol < WINDOW, past - WINDOW + wcol, past + wcol - WINDOW)
        dist_w = qpos - wpos
        ok = (dist_w >= 0) & (dist_w <= WINDOW) & (wcol < WINDOW + SPAD)
        bias_w[...] = jnp.where(ok, -slope * dist_w.astype(F32), NEG)
        ej = lax.broadcasted_iota(I32, (64, nsel), 0)
        ek = lax.shift_right_logical(lax.broadcasted_iota(I32, (64, nsel), 1), 6)
        ebig[...] = jnp.where(ej == ek, NEG, 0.0).astype(BF16)

    @pl.when(b + 1 < nb)
    def _():
        start_seq(b + 1, 1 - slot)

    for p in range(npages):
        page_copy(b, p, slot).wait()

    cmp_out = []
    for w in range(4):
        s, hf = divmod(w, 2)
        tok = pbuf[slot, s, hf * LANES:(hf + 1) * LANES, :].T
        tok3 = tok.reshape(nchunk, CMP_STRIDE, LANES)
        cmplo[w] = tok3[:, 0:8, :].reshape(nchunk * 8, LANES)
        cmphi[w] = tok3[:, 8:16, :].reshape(nchunk * 8, LANES)

        def load_x(l, w=w):
            src = cmplo if l < 8 else cmphi
            return src[w, pl.ds(l % 8, nchunk, stride=8), :]

        hids = _compress_hidden(load_x, w1q_ref.at[s], c_ref[s, 0:1, :], nchunk)
        out = _dot(hids[0], w2h_ref[s, 0]) + _dot(hids[1], w2h_ref[s, 1])
        rix = lax.broadcasted_iota(I32, (nchunk, LANES), 0)
        cmp_out.append(jnp.where(rix < nchunk - 1, out, 0.0))
    kc = jnp.concatenate(cmp_out[0:2], axis=1).astype(BF16)
    vc = jnp.concatenate(cmp_out[2:4], axis=1).astype(BF16)

    qf = q_ref[...].astype(F32)
    lane4 = lax.broadcasted_iota(I32, (SPAD, D_KV), 1)
    pieces = []
    for hd in range(N_HEADS):
        g, h = divmod(hd, HEADS_PER_GROUP)
        w = qf[:, g * D_KV:(g + 1) * D_KV]
        sh = ((g - h) * HEAD_DIM) % D_KV
        if sh:
            w = pltpu.roll(w, sh, 1)
        inside = (lane4 >= g * HEAD_DIM) & (lane4 < (g + 1) * HEAD_DIM)
        pieces.append(jnp.where(inside, w, 0.0))
    qbd = jnp.concatenate(pieces, axis=0).astype(BF16)

    def new_rows(c):
        new = kvn_ref[:, c * D_KV:(c + 1) * D_KV].astype(F32)
        return jnp.concatenate([new, jnp.zeros((LANES - SPAD, D_KV), F32)], axis=0).astype(BF16)

    s_c = _dot_nt(qbd, kc) + bias_c[...]
    m_c = jnp.max(s_c, axis=-1, keepdims=True)
    e_c = jnp.exp2(s_c - m_c)
    p_c = e_c / jnp.maximum(jnp.sum(e_c, axis=-1, keepdims=True), 1e-30)
    o_c = _dot(p_c.astype(BF16), vc)

    psum = jnp.concatenate(
        [sum(p_c[(g * HEADS_PER_GROUP + h) * SPAD:(g * HEADS_PER_GROUP + h + 1) * SPAD]
             for h in range(HEADS_PER_GROUP)) for g in range(N_KV_GROUPS)], axis=0)
    cj = lax.broadcasted_iota(I32, (nchunk, 64), 0)
    jj = lax.broadcasted_iota(I32, (nchunk, 64), 1)
    ov = jnp.where(CMP_STRIDE * cj < SEL_BLOCK * (jj + 1),
                   jnp.where(CMP_STRIDE * cj + CMP_BLOCK - 1 >= SEL_BLOCK * jj, 1.0, 0.0),
                   0.0).astype(BF16)
    p_hi = psum.astype(BF16)
    p_lo = (psum - p_hi.astype(F32)).astype(BF16)
    imp = _dot(p_hi, ov) + _dot(p_lo, ov)
    nrow = N_KV_GROUPS * SPAD
    jb = lax.broadcasted_iota(I32, (nrow, 64), 1)
    cur = lax.shift_right_logical(
        past + jnp.bitwise_and(lax.broadcasted_iota(I32, (nrow, 64), 0), SPAD - 1), 6)
    elig = jb <= cur
    forced = (jb == 0) | (jb == cur) | (jb == cur - 1)
    sc = jnp.where(elig, jnp.where(forced, FORCE_SCORE, imp), NEG)
    rank = jnp.zeros((nrow, 64), I32)
    for ii in range(past // SEL_BLOCK + 1):
        col = sc[:, ii:ii + 1]
        rank = rank + jnp.where(col > sc, 1, jnp.where(col == sc, jnp.where(ii < jb, 1, 0), 0))
    neg = jnp.where(elig, jnp.where(rank < N_SELECT, 0.0, 1.0), 1.0)
    neg_rows = jnp.concatenate(
        [neg[(hd // HEADS_PER_GROUP) * SPAD:(hd // HEADS_PER_GROUP + 1) * SPAD] for hd in range(N_HEADS)],
        axis=0).astype(BF16)

    k_new, v_new = new_rows(0), new_rows(1)
    s_s = (jnp.concatenate([_dot(qbd, pbuf[slot, 2].astype(BF16)), _dot_nt(qbd, k_new)], axis=1)
           + _dot(neg_rows, ebig[...]) + bias_s[...])
    m_s = jnp.max(s_s, axis=-1, keepdims=True)
    e_s = jnp.exp2(s_s - m_s)
    e_sb = e_s.astype(BF16)
    o_s = (_dot_nt(e_sb[:, 0:past], pbuf[slot, 3].astype(BF16)) + _dot(e_sb[:, past:], v_new)
           ) / jnp.sum(e_s, axis=-1, keepdims=True)

    kw_new, vw_new = new_rows(2), new_rows(3)
    s_w = jnp.concatenate([_dot(qbd, win_ref[0].astype(BF16)), _dot_nt(qbd, kw_new)], axis=1) + bias_w[...]
    m_w = jnp.max(s_w, axis=-1, keepdims=True)
    e_w = jnp.exp2(s_w - m_w)
    e_wb = e_w.astype(BF16)
    o_w = (_dot_nt(e_wb[:, 0:WINDOW], win_ref[1].astype(BF16)) + _dot(e_wb[:, WINDOW:], vw_new)
           ) / jnp.sum(e_w, axis=-1, keepdims=True)

    gates = gate_ref[...]
    gcol = lambda br: jnp.concatenate(
        [gates[:, 3 * hd + br:3 * hd + br + 1] for hd in range(N_HEADS)], axis=0)
    o_all = gcol(0) * o_c + gcol(1) * o_s + gcol(2) * o_w

    outs = []
    for g in range(N_KV_GROUPS):
        inside = (lane4 >= g * HEAD_DIM) & (lane4 < (g + 1) * HEAD_DIM)
        tot = jnp.zeros((SPAD, D_KV), F32)
        for h in range(HEADS_PER_GROUP):
            hd = g * HEADS_PER_GROUP + h
            piece = jnp.where(inside, o_all[hd * SPAD:(hd + 1) * SPAD], 0.0)
            sh = ((h - g) * HEAD_DIM) % D_KV
            if sh:
                piece = pltpu.roll(piece, sh, 1)
            tot = tot + piece
        outs.append(tot)
    o_ref[...] = jnp.concatenate(outs, axis=1) * zg_ref[...]


def nsa_sample_attn(page_table, q, kvn, gates, zg, win_t, w1q, w2h, const, cache_t, *, layer):
    nb, npages = page_table.shape
    past = npages * PAGE_SIZE
    nchunk = past // CMP_STRIDE
    rows = N_HEADS * SPAD
    seq = lambda i, pt: (i, 0)
    grid_spec = pltpu.PrefetchScalarGridSpec(
        num_scalar_prefetch=1,
        grid=(nb,),
        in_specs=[pl.BlockSpec((SPAD, D_MODEL), seq),
                  pl.BlockSpec((SPAD, 4 * D_KV), seq),
                  pl.BlockSpec((SPAD, LANES), seq),
                  pl.BlockSpec((SPAD, D_MODEL), seq),
                  pl.BlockSpec((None, None, 2, D_KV, WINDOW), lambda i, pt: (layer, i, 0, 0, 0)),
                  pl.BlockSpec((2, 4, 256, 256), lambda i, pt: (0, 0, 0, 0)),
                  pl.BlockSpec((2, 2, CMP_HIDDEN, LANES), lambda i, pt: (0, 0, 0, 0)),
                  pl.BlockSpec((2, 8, CMP_HIDDEN), lambda i, pt: (0, 0, 0)),
                  pl.BlockSpec(memory_space=pl.ANY)],
        out_specs=pl.BlockSpec((SPAD, D_MODEL), seq),
        scratch_shapes=[pltpu.VMEM((2, 4, D_KV, past), F32),
                        pltpu.SemaphoreType.DMA((2,)),
                        pltpu.VMEM((4, nchunk * 8, LANES), F32),
                        pltpu.VMEM((4, nchunk * 8, LANES), F32),
                        pltpu.VMEM((64, past + LANES), BF16),
                        pltpu.VMEM((rows, nchunk), F32),
                        pltpu.VMEM((rows, past + LANES), F32),
                        pltpu.VMEM((rows, WINDOW + LANES), F32)])
    return pl.pallas_call(
        functools.partial(_nsa_sample_body, layer=layer, npages=npages),
        grid_spec=grid_spec,
        out_shape=jax.ShapeDtypeStruct((nb * SPAD, D_MODEL), F32),
        compiler_params=_params("arbitrary"),
        name="nsa_sample_attn",
    )(page_table, q, kvn, gates, zg, win_t, w1q, w2h, const, cache_t)


def _outproj_body(o_ref, x_ref, w_ref, fg_ref, y_ref, *, final):
    xn = x_ref[...] + _dot(o_ref[...].astype(BF16), w_ref[...])
    if final:
        xn = _rms(xn, fg_ref[...])
    y_ref[...] = xn


def outproj(o, x, w_bf, fg, *, final, tm=256):
    n = o.shape[0]
    row = lambda i: (i, 0)
    return pl.pallas_call(
        functools.partial(_outproj_body, final=final),
        grid=(n // tm,),
        in_specs=[pl.BlockSpec((tm, D_MODEL), row), pl.BlockSpec((tm, D_MODEL), row),
                  pl.BlockSpec((D_MODEL, D_MODEL), lambda i: (0, 0)),
                  pl.BlockSpec((1, D_MODEL), lambda i: (0, 0))],
        out_specs=pl.BlockSpec((tm, D_MODEL), row),
        out_shape=jax.ShapeDtypeStruct((n, D_MODEL), F32),
        compiler_params=_params("arbitrary"),
        name="outproj",
    )(o, x, w_bf, fg)


def _prep_cmp_weights(pe, w1, w2):
    top = w1[:, :CMP_STRIDE].reshape(2, 4, 4 * HEAD_DIM, CMP_HIDDEN)
    bot = w1[:, CMP_STRIDE:].reshape(2, 4, 4 * HEAD_DIM, CMP_HIDDEN)
    w1q = jnp.concatenate([top, bot], axis=-1).astype(BF16)
    w2h = jnp.zeros((2, 2, CMP_HIDDEN, LANES), F32)
    for hf in range(2):
        w2h = w2h.at[:, hf, :, hf * HEAD_DIM:(hf + 1) * HEAD_DIM].set(w2)
    w2t = jnp.swapaxes(w2h, 2, 3)
    pe_pad = jnp.zeros((2, 8, CMP_BLOCK * HEAD_DIM), F32).at[:, 0].set(pe.reshape(2, -1))
    w1_flat = w1.reshape(2, CMP_BLOCK * HEAD_DIM, CMP_HIDDEN)
    return w1q, w2h.astype(BF16), w2t.astype(BF16), pe_pad.astype(BF16), w1_flat.astype(BF16)


def _prep_nsa_in_weights(w_in):
    gl = jnp.pad(w_in[:, 3584:], ((0, 0), (0, LANES - N_BRANCH * N_HEADS)))
    w_a = jnp.concatenate([w_in[:, 0:2048], w_in[:, 2048:2560], gl], axis=1).astype(BF16)
    w_t = w_in[:, 2048:3584].T.astype(BF16)
    w_n = w_in[:, 2560:3584].astype(BF16)
    return w_a, w_t, w_n


def kernel(x_prompt, x_sample, cache_kv, state_kv_win, state_conv, page_table, norm_g, final_norm_g,
           conv_w_in, conv_dw_w, conv_dw_b, conv_ln_g, conv_ln_b, conv_w_out,
           nsa_w_in, nsa_w_out, nsa_cmp_pe, nsa_cmp_w1, nsa_cmp_w2):
    bsz, seq, _ = x_prompt.shape
    nb = x_sample.shape[0]
    n_nsa = nsa_w_in.shape[0]
    n_pool = cache_kv.shape[1]
    win_keep = state_kv_win.shape[2]
    ns = SPAD * nb
    xp = x_prompt
    xs = jnp.pad(jnp.swapaxes(x_sample, 0, 1), ((0, SPAD - DEC_SEQ), (0, 0), (0, 0)))
    cache_t = jnp.transpose(cache_kv, (0, 1, 3, 4, 5, 2)).reshape(n_nsa, n_pool, 4, D_KV, PAGE_SIZE)
    win_state_t = jnp.transpose(state_kv_win, (0, 1, 3, 4, 5, 2)).reshape(n_nsa, nb, 2, D_KV, win_keep)
    conv_state_t = jnp.swapaxes(state_conv, 1, 2)
    fg = final_norm_g.reshape(1, D_MODEL)

    kv_p, kv_s, win_p, win_s, conv_p, conv_s = [], [], [], [], [], []
    for layer in range(DEPTH):
        j = layer // 2
        g = norm_g[layer].reshape(1, D_MODEL)
        last = layer == DEPTH - 1
        if layer % 2 == 0:
            w_in = conv_w_in[j].astype(BF16)
            w_out = conv_w_out[j].astype(BF16)
            db = conv_dw_b[j].reshape(1, D_CONV)
            lg = conv_ln_g[j].reshape(1, D_CONV)
            lb = conv_ln_b[j].reshape(1, D_CONV)
            v, zg = conv_inproj(xp.reshape(bsz * seq, D_MODEL), g, w_in)
            v = v.reshape(bsz, seq, D_CONV)
            xp = conv_prompt(v, zg.reshape(bsz, seq, D_CONV), xp, conv_dw_w[j], db, lg, lb, w_out)
            conv_p.append(v[:, seq - CONV_STATE:])
            vs, zgs = conv_inproj(xs.reshape(ns, D_MODEL), g, w_in)
            vs = vs.reshape(SPAD, nb, D_CONV)
            xs = conv_sample(conv_state_t, vs, zgs.reshape(SPAD, nb, D_CONV), xs, conv_dw_w[j],
                             db, lg, lb, w_out, layer=j)
            conv_s.append(vs[:DEC_SEQ])
        else:
            w_a, w_t, w_n = _prep_nsa_in_weights(nsa_w_in[j])
            w_out = nsa_w_out[j].astype(BF16)
            w1q, w2h, w2t, pe_pad, w1_flat = _prep_cmp_weights(nsa_cmp_pe[j], nsa_cmp_w1[j], nsa_cmp_w2[j])
            const = cmp_const(pe_pad, w1_flat)
            q, zg, gates, kvpt, kvwt, kcn, kvtb = nsa_inproj(xp, g, w_a, w_t)
            cmpt = compress_prompt(kcn, w1q, w2t, const)
            xp = nsa_prompt_attn(q, kvtb, cmpt, gates, zg, xp, w_out, fg, final=last)
            kv_p.append(jnp.transpose(kvpt.reshape(bsz, 4, N_KV_GROUPS, HEAD_DIM, seq), (0, 4, 1, 2, 3)))
            wtail = kvwt.reshape(bsz, 2, N_KV_GROUPS, HEAD_DIM, seq)[..., seq - min(WINDOW, seq):]
            win_p.append(jnp.transpose(wtail, (0, 4, 1, 2, 3)))
            xq = jnp.swapaxes(xs, 0, 1).reshape(ns, D_MODEL)
            qs, zgs, gs, kvpts, kvwts, kvns = nsa_inproj(xq.reshape(1, ns, D_MODEL), g, w_a, w_t, w_n)
            og = nsa_sample_attn(page_table, qs.reshape(ns, D_MODEL), kvns.reshape(ns, 4 * D_KV),
                                 gs.reshape(ns, LANES), zgs.reshape(ns, D_MODEL),
                                 win_state_t, w1q, w2h, const, cache_t, layer=j)
            xq = outproj(og, xq, w_out, fg, final=last)
            xs = jnp.swapaxes(xq.reshape(nb, SPAD, D_MODEL), 0, 1)
            kvps5 = kvpts.reshape(4, N_KV_GROUPS, HEAD_DIM, nb, SPAD)[..., :DEC_SEQ]
            kv_s.append(jnp.transpose(kvps5, (3, 4, 0, 1, 2)))
            kvws5 = kvwts.reshape(2, N_KV_GROUPS, HEAD_DIM, nb, SPAD)[..., :DEC_SEQ]
            new_w = jnp.transpose(kvws5, (3, 4, 0, 1, 2))
            win_s.append(new_w)
    y_sample = jnp.swapaxes(xs[:DEC_SEQ], 0, 1)
    win_sample = jnp.concatenate([state_kv_win, jnp.stack(win_s)], axis=2)[:, :, -win_keep:]
    conv_sample_out = jnp.swapaxes(
        jnp.concatenate([conv_state_t[:, DEC_SEQ:], jnp.stack(conv_s)], axis=1), 1, 2)
    return (xp, y_sample, jnp.stack(kv_p), jnp.stack(kv_s), jnp.stack(win_p), win_sample,
            jnp.stack(conv_p), conv_sample_out)
```
